```python
import math
import jax, jax.numpy as jnp
from jax import lax
import numpy as np

D_MODEL = 1024
BATCH = 2
SEQ = 8192
DEPTH = 1

CHUNK = 64
GMLP_BLOCK = 128
SB_QBLOCK = 128
MIX_WIDTH = D_MODEL
WIDTH_A = MIX_WIDTH // 2
WIDTH_B = MIX_WIDTH - WIDTH_A
HEAD_DIM = 64
HEADS_A = WIDTH_A // HEAD_DIM
HEADS_B = WIDTH_B // HEAD_DIM
IN_WIDTH = 3 * WIDTH_A + 4 * WIDTH_B
DEEPNORM_ALPHA = (2 * DEPTH) ** 0.25
DEEPNORM_BETA = (8 * DEPTH) ** -0.25
LN_EPS = 1e-5

kernel_name = "hybrid_gmlp_stickbreaking_deepnorm_adaln"


def layer_norm(x, g, b):
    xf = x.astype(jnp.float32)
    mu = jnp.mean(xf, axis=-1, keepdims=True)
    var = jnp.mean(jnp.square(xf - mu), axis=-1, keepdims=True)
    y = (xf - mu) * lax.rsqrt(var + LN_EPS)
    return (y * g.astype(jnp.float32) + b.astype(jnp.float32)).astype(x.dtype)


def chunk_causal_block_mask():
    i = jnp.arange(GMLP_BLOCK) // CHUNK
    return i[None, :] <= i[:, None]


def gmlp_spatial_gating(u, v, ln_g, ln_b, w_s, b_s):
    B, S, _ = v.shape
    nblk = S // GMLP_BLOCK
    v = layer_norm(v, ln_g, ln_b)
    vb = v.reshape(B, nblk, GMLP_BLOCK, HEADS_A, HEAD_DIM)
    w = jnp.where(chunk_causal_block_mask()[None], w_s, jnp.zeros_like(w_s))
    mixed = jnp.einsum('hij,bnjhd->bnihd', w, vb) + jnp.transpose(b_s)[None, None, :, :, None]
    return u * mixed.reshape(B, S, WIDTH_A)


def stick_breaking_attention(q, k, v):
    S = q.shape[2]
    dh = q.shape[3]
    scale = 1.0 / math.sqrt(dh)
    outs = []
    for qb in range(S // SB_QBLOCK):
        start = qb * SB_QBLOCK
        end = start + SB_QBLOCK
        kk = k[:, :, :end]
        vv = v[:, :, :end]
        z = jnp.einsum('bhqd,bhkd->bhqk', q[:, :, start:end], kk,
                       preferred_element_type=jnp.float32) * scale
        t_pos = start + jnp.arange(SB_QBLOCK)[:, None]
        s_pos = jnp.arange(end)[None, :]
        valid = s_pos < t_pos
        log_keep = jnp.where(valid, -jax.nn.softplus(z), 0.0)
        after = lax.cumsum(log_keep, axis=3, reverse=True) - log_keep
        log_w = jax.nn.log_sigmoid(z) + after
        w = jnp.where(valid, jnp.exp(log_w), 0.0)
        outs.append(jnp.einsum('bhqk,bhkd->bhqd', w.astype(v.dtype), vv))
    return jnp.concatenate(outs, axis=2)


def setup_inputs(seed: int = 0) -> dict:
    key = jax.random.key(seed)
    ks = jax.random.split(key, 12)
    f32 = jnp.float32
    x = jax.random.normal(ks[0], (BATCH, SEQ, D_MODEL), f32)
    c = jax.random.normal(ks[1], (BATCH, D_MODEL), f32)
    w_ada = jax.random.normal(ks[2], (DEPTH, D_MODEL, 3 * D_MODEL), f32) * (0.1 * D_MODEL ** -0.5)
    b_ada = jax.random.normal(ks[3], (DEPTH, 3 * D_MODEL), f32) * 0.02
    w_in = jax.random.normal(ks[4], (DEPTH, D_MODEL, IN_WIDTH), f32) * D_MODEL ** -0.5
    ln_v_g = 1.0 + 0.02 * jax.random.normal(ks[5], (DEPTH, WIDTH_A), f32)
    ln_v_b = 0.02 * jax.random.normal(ks[6], (DEPTH, WIDTH_A), f32)
    w_spatial = jax.random.normal(ks[7], (DEPTH, HEADS_A, GMLP_BLOCK, GMLP_BLOCK), f32) * GMLP_BLOCK ** -0.5
    b_spatial = 1.0 + 0.02 * jax.random.normal(ks[8], (DEPTH, HEADS_A, GMLP_BLOCK), f32)
    w_out = jax.random.normal(ks[9], (DEPTH, MIX_WIDTH, D_MODEL), f32) * (MIX_WIDTH ** -0.5 * DEEPNORM_BETA)
    ln_g = 1.0 + 0.02 * jax.random.normal(ks[10], (DEPTH, D_MODEL), f32)
    ln_b = 0.02 * jax.random.normal(ks[11], (DEPTH, D_MODEL), f32)
    return {"x": x, "c": c, "w_ada": w_ada, "b_ada": b_ada, "w_in": w_in,
            "ln_v_g": ln_v_g, "ln_v_b": ln_v_b, "w_spatial": w_spatial, "b_spatial": b_spatial,
            "w_out": w_out, "ln_g": ln_g, "ln_b": ln_b}


def reference(x, c, w_ada, b_ada, w_in, ln_v_g, ln_v_b, w_spatial, b_spatial, w_out, ln_g, ln_b):
    B, S, D = x.shape
    for l in range(DEPTH):
        mod = c @ w_ada[l] + b_ada[l]
        shift, scale, gate = jnp.split(mod, 3, axis=-1)
        h = x * (1.0 + scale[:, None, :]) + shift[:, None, :]
        proj = h @ w_in[l]
        u_a, v_a, z_a, q_b, k_b, v_b, z_b = jnp.split(
            proj, np.cumsum([WIDTH_A, WIDTH_A, WIDTH_A, WIDTH_B, WIDTH_B, WIDTH_B]).tolist(), axis=-1)
        out_a = gmlp_spatial_gating(jax.nn.gelu(u_a), jax.nn.gelu(v_a),
                                    ln_v_g[l], ln_v_b[l], w_spatial[l], b_spatial[l])
        out_a = out_a * jax.nn.silu(z_a)
        to_heads = lambda t: jnp.transpose(t.reshape(B, S, HEADS_B, HEAD_DIM), (0, 2, 1, 3))
        o_b = stick_breaking_attention(to_heads(q_b), to_heads(k_b), to_heads(v_b))
        out_b = jnp.transpose(o_b, (0, 2, 1, 3)).reshape(B, S, WIDTH_B) * jax.nn.silu(z_b)
        y = jnp.concatenate([out_a, out_b], axis=-1) @ w_out[l]
        x = layer_norm(DEEPNORM_ALPHA * x + (1.0 + gate[:, None, :]) * y, ln_g[l], ln_b[l])
    return x
```

```python
import functools
import math

import jax
import jax.numpy as jnp
from jax import lax
from jax.experimental import pallas as pl
from jax.experimental.pallas import tpu as pltpu

HEAD_DIM = 64
CHUNK = 64
BLK = 128
LN_EPS = 1e-5

LANES = 128
ROWS_F32 = 8
ROWS_BF16 = 16
VMEM_LIMIT_BYTES = 56 * 1024 * 1024

PROJ_ROWS = 512
ATTN_QBLOCKS = 2
ATTN_STATIC_BLOCKS = 3
ATTN_LOG_KEEP_FLOOR = -110.0

_F32 = jnp.float32
_BF16 = jnp.bfloat16
_NT = (((1,), (1,)), ((), ()))


def _layer_norm_rows(v, g, b):
    mu = jnp.mean(v, axis=-1, keepdims=True)
    d = v - mu
    var = jnp.mean(d * d, axis=-1, keepdims=True)
    return d * lax.rsqrt(var + LN_EPS) * g + b


def _adaln_kernel(c_ref, w_ref, b_ref, o_ref):
    o_ref[...] = jnp.dot(c_ref[...].astype(_BF16), w_ref[...].astype(_BF16),
                         preferred_element_type=_F32) + b_ref[...]


def _adaln_mod(c_pad, w_ada, b_ada):
    rows, d = c_pad.shape
    n = w_ada.shape[1]
    tn = d
    return pl.pallas_call(
        _adaln_kernel,
        grid=(n // tn,),
        in_specs=[pl.BlockSpec((rows, d), lambda j: (0, 0)),
                  pl.BlockSpec((d, tn), lambda j: (0, j)),
                  pl.BlockSpec((1, tn), lambda j: (0, j))],
        out_specs=pl.BlockSpec((rows, tn), lambda j: (0, j)),
        out_shape=jax.ShapeDtypeStruct((rows, n), _F32),
        compiler_params=pltpu.CompilerParams(vmem_limit_bytes=VMEM_LIMIT_BYTES),
        name="adaln_mod",
    )(c_pad, w_ada, b_ada)


def _inproj_kernel(x_ref, mod_ref, w_ref, wt_ref, lng_ref, lnb_ref, ws_ref, bs_ref,
                   oa_ref, qt_ref, k_ref, vt_ref, sb_ref, *, d_model, width):
    tm = x_ref.shape[0]
    mod = mod_ref[0]
    shift = mod[:, :d_model]
    scale = mod[:, d_model:2 * d_model]
    h = (x_ref[...] * (1.0 + scale) + shift).astype(_BF16)

    def group(g):
        return jnp.dot(h, w_ref[:, g * width:(g + 1) * width], preferred_element_type=_F32)

    u = jax.nn.gelu(group(0))
    vn = _layer_norm_rows(jax.nn.gelu(group(1)), lng_ref[...], lnb_ref[...]).astype(_BF16)
    za = group(2)
    gate_a = u * (za * jax.nn.sigmoid(za))

    row = lax.broadcasted_iota(jnp.int32, (BLK, BLK), 0)
    col = lax.broadcasted_iota(jnp.int32, (BLK, BLK), 1)
    causal = (col // CHUNK) <= (row // CHUNK)
    lane = lax.broadcasted_iota(jnp.int32, (BLK, LANES), 1)
    first_head = lane < HEAD_DIM
    heads_per_lane_group = LANES // HEAD_DIM
    for p in range(width // LANES):
        w_pair = jnp.concatenate(
            [jnp.where(causal, ws_ref[heads_per_lane_group * p + i], 0.0).astype(_BF16)
             for i in range(heads_per_lane_group)], axis=0)
        cols = slice(p * LANES, (p + 1) * LANES)
        for r in range(tm // BLK):
            rows = slice(r * BLK, (r + 1) * BLK)
            res = jnp.dot(w_pair, vn[rows, cols], preferred_element_type=_F32)
            mixed = jnp.where(first_head, res[:BLK], res[BLK:]) + bs_ref[:, cols]
            oa_ref[rows, cols] = (gate_a[rows, cols] * mixed).astype(_BF16)

    k_ref[...] = group(3).astype(_BF16)
    zb = group(4)
    sb_ref[...] = zb * jax.nn.sigmoid(zb)
    qv_t = lax.dot_general(wt_ref[...], h, _NT, preferred_element_type=_F32)
    q_scale = 1.0 / math.sqrt(HEAD_DIM)
    for r in range(tm // BLK):
        cols = slice(r * BLK, (r + 1) * BLK)
        qt_ref[0, r] = (qv_t[:width, cols] * q_scale).astype(_BF16)
        vt_ref[0, r] = qv_t[width:, cols].astype(_BF16)


def _in_proj(x2, mod3, w_main, w_qv_t, ln_v_g, ln_v_b, w_spatial, bias_tile, *, batch, seq):
    m, d_model = x2.shape
    width = ln_v_g.shape[-1]
    tm = PROJ_ROWS
    steps_per_batch = seq // tm
    nblk = seq // BLK
    row_spec = lambda cols: pl.BlockSpec((tm, cols), lambda i: (i, 0))
    full = lambda a: pl.BlockSpec(a.shape, lambda i: (0,) * a.ndim)
    t_spec = pl.BlockSpec((1, tm // BLK, width, BLK),
                          lambda i: (i // steps_per_batch, i % steps_per_batch, 0, 0))
    return pl.pallas_call(
        functools.partial(_inproj_kernel, d_model=d_model, width=width),
        grid=(m // tm,),
        in_specs=[row_spec(d_model),
                  pl.BlockSpec((1, 1, mod3.shape[-1]), lambda i: (i // steps_per_batch, 0, 0)),
                  full(w_main), full(w_qv_t), full(ln_v_g), full(ln_v_b), full(w_spatial),
                  full(bias_tile)],
        out_specs=[row_spec(width), t_spec, row_spec(width), t_spec, row_spec(width)],
        out_shape=[jax.ShapeDtypeStruct((m, width), _BF16),
                   jax.ShapeDtypeStruct((batch, nblk, width, BLK), _BF16),
                   jax.ShapeDtypeStruct((m, width), _BF16),
                   jax.ShapeDtypeStruct((batch, nblk, width, BLK), _BF16),
                   jax.ShapeDtypeStruct((m, width), _F32)],
        compiler_params=pltpu.CompilerParams(vmem_limit_bytes=VMEM_LIMIT_BYTES),
        name="in_proj",
    )(x2, mod3, w_main, w_qv_t, ln_v_g, ln_v_b, w_spatial, bias_tile)


def _sb_block(k_pair, qt_head, vt_head, carry, cum_t, valid):
    z = jnp.dot(k_pair, qt_head, preferred_element_type=_F32)
    softplus = jnp.maximum(z, 0.0) + jnp.log(1.0 + jnp.exp(-jnp.abs(z)))
    log_keep = -softplus
    log_beta = z + log_keep
    if valid is not None:
        log_keep = jnp.where(valid, log_keep, 0.0)
    hi = log_keep.astype(_BF16)
    lo = (log_keep - hi.astype(_F32)).astype(_BF16)
    cs = jnp.dot(cum_t, jnp.concatenate([hi, lo], axis=1), preferred_element_type=_F32)
    cs = cs[:, :LANES] + cs[:, LANES:]
    after = cs[:BLK] + carry[0:1, :]
    w = jnp.exp(log_beta + after)
    if valid is not None:
        w = jnp.where(valid, w, 0.0)
    pv = jnp.dot(vt_head, w.astype(_BF16), preferred_element_type=_F32)
    return pv, cs[BLK:BLK + ROWS_F32]


def _attn_kernel(qt_ref, k_ref, vt_ref, sb_ref, o_ref, acc_ref, car_ref, *, exit_floor):
    nqb = qt_ref.shape[1]
    width = qt_ref.shape[2]
    n_heads = width // HEAD_DIM
    heads_per_lane_group = LANES // HEAD_DIM
    first_qblock = pl.program_id(1) * nqb

    row = lax.broadcasted_iota(jnp.int32, (BLK + ROWS_BF16, BLK), 0)
    col = lax.broadcasted_iota(jnp.int32, (BLK + ROWS_BF16, BLK), 1)
    cum_t = jnp.where((col > row) | (row >= BLK), 1.0, 0.0).astype(_BF16)
    key_before_query = (lax.broadcasted_iota(jnp.int32, (BLK, BLK), 0)
                        < lax.broadcasted_iota(jnp.int32, (BLK, BLK), 1))
    neg_inf = jnp.full((ROWS_F32, LANES), -jnp.inf, _F32)
    zero_rows = jnp.zeros((HEAD_DIM, LANES), _BF16)

    def head_operands(qb, head):
        p, i = divmod(head, heads_per_lane_group)
        lanes = slice(p * LANES, (p + 1) * LANES)
        q_rows = qt_ref[0, qb, head * HEAD_DIM:(head + 1) * HEAD_DIM, :]
        parts = [zero_rows] * heads_per_lane_group
        parts[i] = q_rows
        return lanes, jnp.concatenate(parts, axis=0)

    def sweep(qb, head, j, carry, valid):
        lanes, qt_head = head_operands(qb, head)
        present = j >= 0
        jc = jnp.maximum(j, 0)
        pv, tot = _sb_block(k_ref[0, jc, :, lanes], qt_head,
                            vt_ref[0, jc, head * HEAD_DIM:(head + 1) * HEAD_DIM, :],
                            carry, cum_t, valid)
        return jnp.where(present, pv, 0.0), carry + jnp.where(present, tot, 0.0)

    live = neg_inf
    for qb in range(nqb):
        i_q = first_qblock + qb
        more = i_q - ATTN_STATIC_BLOCKS >= 0
        for head in range(n_heads):
            carry = jnp.zeros((ROWS_F32, LANES), _F32)
            acc = jnp.zeros((HEAD_DIM, LANES), _F32)
            for d in range(ATTN_STATIC_BLOCKS):
                pv, carry = sweep(qb, head, i_q - d, carry, key_before_query if d == 0 else None)
                acc = acc + pv
            acc_ref[qb, head * HEAD_DIM:(head + 1) * HEAD_DIM, :] = acc
            car_ref[qb * n_heads + head] = carry
            live = jnp.maximum(live, jnp.where(more, carry, neg_inf))

    def tail_cond(state):
        return state[1] > exit_floor

    def tail_body(state):
        n = state[0]
        live = neg_inf
        for qb in range(nqb):
            j = first_qblock + qb - ATTN_STATIC_BLOCKS - n
            more = j >= 1
            for head in range(n_heads):
                rows = slice(head * HEAD_DIM, (head + 1) * HEAD_DIM)
                pv, carry = sweep(qb, head, j, car_ref[qb * n_heads + head], None)
                acc_ref[qb, rows, :] += pv
                car_ref[qb * n_heads + head] = carry
                live = jnp.maximum(live, jnp.where(more, carry, neg_inf))
        return n + 1, jnp.max(live)

    lax.while_loop(tail_cond, tail_body, (jnp.int32(0), jnp.max(live)))

    for qb in range(nqb):
        rows = slice(qb * BLK, (qb + 1) * BLK)
        for p in range(width // LANES):
            lanes = slice(p * LANES, (p + 1) * LANES)
            o_ref[0, rows, lanes] = (acc_ref[qb, lanes, :].T * sb_ref[0, rows, lanes]).astype(_BF16)


def _sb_attention(q_t, k_blocks, v_t, sb3, *, exit_floor=ATTN_LOG_KEEP_FLOOR):
    batch, nblk, width, _ = q_t.shape
    nqb = ATTN_QBLOCKS
    tq = nqb * BLK
    n_heads = width // HEAD_DIM
    resident = lambda a: pl.BlockSpec((1,) + a.shape[1:], lambda b, i: (b, 0, 0, 0))
    return pl.pallas_call(
        functools.partial(_attn_kernel, exit_floor=exit_floor),
        grid=(batch, nblk // nqb),
        in_specs=[pl.BlockSpec((1, nqb, width, BLK), lambda b, i: (b, i, 0, 0)),
                  resident(k_blocks), resident(v_t),
                  pl.BlockSpec((1, tq, width), lambda b, i: (b, i, 0))],
        out_specs=pl.BlockSpec((1, tq, width), lambda b, i: (b, i, 0)),
        out_shape=jax.ShapeDtypeStruct(sb3.shape, _BF16),
        scratch_shapes=[pltpu.VMEM((nqb, width, BLK), _F32),
                        pltpu.VMEM((nqb * n_heads, ROWS_F32, LANES), _F32)],
        compiler_params=pltpu.CompilerParams(
            dimension_semantics=("arbitrary", "arbitrary"),
            vmem_limit_bytes=VMEM_LIMIT_BYTES),
        name="sb_attn",
    )(q_t, k_blocks, v_t, sb3)


def _outproj_kernel(oa_ref, ob_ref, x_ref, mod_ref, w_ref, g_ref, b_ref, o_ref, *, alpha):
    width = oa_ref.shape[1]
    d_model = x_ref.shape[1]
    y = (jnp.dot(oa_ref[...], w_ref[:width, :], preferred_element_type=_F32)
         + jnp.dot(ob_ref[...], w_ref[width:, :], preferred_element_type=_F32))
    gate = mod_ref[0][:, 2 * d_model:]
    r = alpha * x_ref[...] + (1.0 + gate) * y
    o_ref[...] = _layer_norm_rows(r, g_ref[...], b_ref[...])


def _out_proj(out_a, out_b, x2, mod3, w_out, ln_g, ln_b, *, seq, alpha):
    m, d_model = x2.shape
    width = out_a.shape[1]
    tm = PROJ_ROWS
    steps_per_batch = seq // tm
    row_spec = lambda cols: pl.BlockSpec((tm, cols), lambda i: (i, 0))
    full = lambda a: pl.BlockSpec(a.shape, lambda i: (0,) * a.ndim)
    return pl.pallas_call(
        functools.partial(_outproj_kernel, alpha=alpha),
        grid=(m // tm,),
        in_specs=[row_spec(width), row_spec(width), row_spec(d_model),
                  pl.BlockSpec((1, 1, mod3.shape[-1]), lambda i: (i // steps_per_batch, 0, 0)),
                  full(w_out), full(ln_g), full(ln_b)],
        out_specs=row_spec(d_model),
        out_shape=jax.ShapeDtypeStruct((m, d_model), _F32),
        compiler_params=pltpu.CompilerParams(vmem_limit_bytes=VMEM_LIMIT_BYTES),
        name="out_proj",
    )(out_a, out_b, x2, mod3, w_out, ln_g, ln_b)


def kernel(x, c, w_ada, b_ada, w_in, ln_v_g, ln_v_b, w_spatial, b_spatial, w_out, ln_g, ln_b):
    batch, seq, d_model = x.shape
    depth = w_ada.shape[0]
    width = ln_v_g.shape[-1]
    n_heads = w_spatial.shape[1]
    assert width == n_heads * HEAD_DIM and w_in.shape[-1] == 7 * width
    assert w_spatial.shape[2:] == (BLK, BLK) and seq % PROJ_ROWS == 0
    assert (seq // BLK) % ATTN_QBLOCKS == 0 and PROJ_ROWS % BLK == 0
    alpha = (2 * depth) ** 0.25

    c_pad = jnp.zeros((ROWS_BF16, d_model), x.dtype).at[:batch].set(c)
    x2 = x.reshape(batch * seq, d_model)
    for l in range(depth):
        grp = lambda g: w_in[l][:, g * width:(g + 1) * width]
        w_main = jnp.concatenate([grp(0), grp(1), grp(2), grp(4), grp(6)], axis=1).astype(_BF16)
        w_qv_t = jnp.concatenate([grp(3), grp(5)], axis=1).T.astype(_BF16)
        bias_tile = jnp.repeat(b_spatial[l].T, HEAD_DIM, axis=1)

        mod = _adaln_mod(c_pad, w_ada[l], b_ada[l][None, :])[:batch]
        mod3 = mod[:, None, :]
        out_a, q_t, k, v_t, sb = _in_proj(
            x2, mod3, w_main, w_qv_t, ln_v_g[l][None, :], ln_v_b[l][None, :], w_spatial[l],
            bias_tile, batch=batch, seq=seq)
        out_b = _sb_attention(q_t, k.reshape(batch, seq // BLK, BLK, width), v_t,
                              sb.reshape(batch, seq, width))
        x2 = _out_proj(out_a, out_b.reshape(batch * seq, width), x2, mod3,
                       w_out[l].astype(_BF16), ln_g[l][None, :], ln_b[l][None, :],
                       seq=seq, alpha=alpha)
    return x2.reshape(batch, seq, d_model)
```

```python
import functools
import math

import jax
import jax.numpy as jnp
from jax import lax
from jax.experimental import pallas as pl
from jax.experimental.pallas import tpu as pltpu

HEAD_DIM = 64
CHUNK = 64
BLK = 128
LN_EPS = 1e-5

LANES = 128
ROWS_F32 = 8
ROWS_BF16 = 16
VMEM_LIMIT_BYTES = 56 * 1024 * 1024

PROJ_ROWS = 512
ATTN_QBLOCKS = 2
ATTN_STATIC_BLOCKS = 3
ATTN_LOG_KEEP_FLOOR = -110.0

_F32 = jnp.float32
_BF16 = jnp.bfloat16
_NT = (((1,), (1,)), ((), ()))


def _layer_norm_rows(v, g, b):
    mu = jnp.mean(v, axis=-1, keepdims=True)
    d = v - mu
    var = jnp.mean(d * d, axis=-1, keepdims=True)
    return d * lax.rsqrt(var + LN_EPS) * g + b


def _adaln_kernel(c_ref, w_ref, b_ref, o_ref):
    o_ref[...] = jnp.dot(c_ref[...].astype(_BF16), w_ref[...].astype(_BF16),
                         preferred_element_type=_F32) + b_ref[...]


def _adaln_mod(c_pad, w_ada, b_ada):
    rows, d = c_pad.shape
    n = w_ada.shape[1]
    tn = d
    return pl.pallas_call(
        _adaln_kernel,
        grid=(n // tn,),
        in_specs=[pl.BlockSpec((rows, d), lambda j: (0, 0)),
                  pl.BlockSpec((d, tn), lambda j: (0, j)),
                  pl.BlockSpec((1, tn), lambda j: (0, j))],
        out_specs=pl.BlockSpec((rows, tn), lambda j: (0, j)),
        out_shape=jax.ShapeDtypeStruct((rows, n), _F32),
        compiler_params=pltpu.CompilerParams(vmem_limit_bytes=VMEM_LIMIT_BYTES),
        name="adaln_mod",
    )(c_pad, w_ada, b_ada)


def _inproj_kernel(x_ref, mod_ref, w_ref, wt_ref, lng_ref, lnb_ref, ws_ref, bs_ref,
                   oa_ref, qt_ref, k_ref, vt_ref, sb_ref, *, d_model, width):
    tm = x_ref.shape[0]
    mod = mod_ref[0]
    shift = mod[:, :d_model]
    scale = mod[:, d_model:2 * d_model]
    h = (x_ref[...] * (1.0 + scale) + shift).astype(_BF16)

    def group(g):
        return jnp.dot(h, w_ref[:, g * width:(g + 1) * width], preferred_element_type=_F32)

    u = jax.nn.gelu(group(0))
    vn = _layer_norm_rows(jax.nn.gelu(group(1)), lng_ref[...], lnb_ref[...]).astype(_BF16)
    za = group(2)
    gate_a = u * (za * jax.nn.sigmoid(za))

    row = lax.broadcasted_iota(jnp.int32, (BLK, BLK), 0)
    col = lax.broadcasted_iota(jnp.int32, (BLK, BLK), 1)
    causal = (col // CHUNK) <= (row // CHUNK)
    lane = lax.broadcasted_iota(jnp.int32, (BLK, LANES), 1)
    first_head = lane < HEAD_DIM
    heads_per_lane_group = LANES // HEAD_DIM
    for p in range(width // LANES):
        w_pair = jnp.concatenate(
            [jnp.where(causal, ws_ref[heads_per_lane_group * p + i], 0.0).astype(_BF16)
             for i in range(heads_per_lane_group)], axis=0)
        cols = slice(p * LANES, (p + 1) * LANES)
        for r in range(tm // BLK):
            rows = slice(r * BLK, (r + 1) * BLK)
            res = jnp.dot(w_pair, vn[rows, cols], preferred_element_type=_F32)
            mixed = jnp.where(first_head, res[:BLK], res[BLK:]) + bs_ref[:, cols]
            oa_ref[rows, cols] = (gate_a[rows, cols] * mixed).astype(_BF16)

    k_ref[...] = group(3).astype(_BF16)
    zb = group(4)
    sb_ref[...] = zb * jax.nn.sigmoid(zb)
    qv_t = lax.dot_general(wt_ref[...], h, _NT, preferred_element_type=_F32)
    q_scale = 1.0 / math.sqrt(HEAD_DIM)
    for r in range(tm // BLK):
        cols = slice(r * BLK, (r + 1) * BLK)
        qt_ref[0, r] = (qv_t[:width, cols] * q_scale).astype(_BF16)
        vt_ref[0, r] = qv_t[width:, cols].astype(_BF16)


def _in_proj(x2, mod3, w_main, w_qv_t, ln_v_g, ln_v_b, w_spatial, bias_tile, *, batch, seq):
    m, d_model = x2.shape
    width = ln_v_g.shape[-1]
    tm = PROJ_ROWS
    steps_per_batch = seq // tm
    nblk = seq // BLK
    row_spec = lambda cols: pl.BlockSpec((tm, cols), lambda i: (i, 0))
    full = lambda a: pl.BlockSpec(a.shape, lambda i: (0,) * a.ndim)
    t_spec = pl.BlockSpec((1, tm // BLK, width, BLK),
                          lambda i: (i // steps_per_batch, i % steps_per_batch, 0, 0))
    return pl.pallas_call(
        functools.partial(_inproj_kernel, d_model=d_model, width=width),
        grid=(m // tm,),
        in_specs=[row_spec(d_model),
                  pl.BlockSpec((1, 1, mod3.shape[-1]), lambda i: (i // steps_per_batch, 0, 0)),
                  full(w_main), full(w_qv_t), full(ln_v_g), full(ln_v_b), full(w_spatial),
                  full(bias_tile)],
        out_specs=[row_spec(width), t_spec, row_spec(width), t_spec, row_spec(width)],
        out_shape=[jax.ShapeDtypeStruct((m, width), _BF16),
                   jax.ShapeDtypeStruct((batch, nblk, width, BLK), _BF16),
                   jax.ShapeDtypeStruct((m, width), _BF16),
                   jax.ShapeDtypeStruct((batch, nblk, width, BLK), _BF16),
                   jax.ShapeDtypeStruct((m, width), _F32)],
        compiler_params=pltpu.CompilerParams(vmem_limit_bytes=VMEM_LIMIT_BYTES),
        name="in_proj",
    )(x2, mod3, w_main, w_qv_t, ln_v_g, ln_v_b, w_spatial, bias_tile)


def _sb_log_terms(z, valid):
    softplus = jnp.maximum(z, 0.0) + jnp.log(1.0 + jnp.exp(-jnp.abs(z)))
    log_keep = -softplus
    log_beta = z + log_keep
    if valid is not None:
        log_keep = jnp.where(valid, log_keep, 0.0)
    hi = log_keep.astype(_BF16)
    lo = (log_keep - hi.astype(_F32)).astype(_BF16)
    return log_beta, jnp.concatenate([hi, lo], axis=1)


def _sb_weights(log_beta, after_in_block, carry, valid):
    w = jnp.exp(log_beta + (after_in_block + carry[0:1, :]))
    if valid is not None:
        w = jnp.where(valid, w, 0.0)
    return w.astype(_BF16)


def _attn_kernel(qt_ref, k_ref, vt_ref, sb_ref, o_ref, acc_ref, car_ref, *, exit_floor):
    nqb = qt_ref.shape[1]
    width = qt_ref.shape[2]
    n_heads = width // HEAD_DIM
    heads_per_lane_group = LANES // HEAD_DIM
    first_qblock = pl.program_id(1) * nqb

    row = lax.broadcasted_iota(jnp.int32, (BLK + ROWS_BF16, BLK), 0)
    col = lax.broadcasted_iota(jnp.int32, (BLK + ROWS_BF16, BLK), 1)
    cum_t = jnp.where((col > row) | (row >= BLK), 1.0, 0.0).astype(_BF16)
    key_before_query = (lax.broadcasted_iota(jnp.int32, (BLK, BLK), 0)
                        < lax.broadcasted_iota(jnp.int32, (BLK, BLK), 1))
    neg_inf = jnp.full((ROWS_F32, LANES), -jnp.inf, _F32)
    zero_rows = jnp.zeros((HEAD_DIM, LANES), _BF16)
    head_rows = lambda head: slice(head * HEAD_DIM, (head + 1) * HEAD_DIM)

    def head_operands(qb, head):
        p, i = divmod(head, heads_per_lane_group)
        parts = [zero_rows] * heads_per_lane_group
        parts[i] = qt_ref[0, qb, head_rows(head), :]
        return slice(p * LANES, (p + 1) * LANES), jnp.concatenate(parts, axis=0)

    def sweep(qb, js, carries, diag_first):
        present = [j >= 0 for j in js]
        jc = [jnp.maximum(j, 0) for j in js]
        always = lambda d: diag_first and d == 0
        valid = lambda d: key_before_query if always(d) else None
        pairs = [(h, d) for h in range(n_heads) for d in range(len(js))]
        operands = [head_operands(qb, h) for h in range(n_heads)]
        z = {(h, d): jnp.dot(k_ref[0, jc[d], :, operands[h][0]], operands[h][1],
                             preferred_element_type=_F32) for h, d in pairs}
        terms = {(h, d): _sb_log_terms(z[h, d], valid(d)) for h, d in pairs}
        cs = {}
        for h, d in pairs:
            both = jnp.dot(cum_t, terms[h, d][1], preferred_element_type=_F32)
            cs[h, d] = both[:, :LANES] + both[:, LANES:]
        w, carries_out = {}, []
        for h in range(n_heads):
            carry = carries[h]
            for d in range(len(js)):
                w[h, d] = _sb_weights(terms[h, d][0], cs[h, d][:BLK], carry, valid(d))
                total = cs[h, d][BLK:BLK + ROWS_F32]
                carry = carry + (total if always(d) else jnp.where(present[d], total, 0.0))
            carries_out.append(carry)
        pv_out = []
        for h in range(n_heads):
            acc = None
            for d in range(len(js)):
                pv = jnp.dot(vt_ref[0, jc[d], head_rows(h), :], w[h, d], preferred_element_type=_F32)
                pv = pv if always(d) else jnp.where(present[d], pv, 0.0)
                acc = pv if acc is None else acc + pv
            pv_out.append(acc)
        return pv_out, carries_out

    live = neg_inf
    zero_carry = jnp.zeros((ROWS_F32, LANES), _F32)
    for qb in range(nqb):
        i_q = first_qblock + qb
        more = i_q - ATTN_STATIC_BLOCKS >= 0
        pv, carries = sweep(qb, [i_q - d for d in range(ATTN_STATIC_BLOCKS)],
                            [zero_carry] * n_heads, True)
        for h in range(n_heads):
            acc_ref[qb, head_rows(h), :] = pv[h]
            car_ref[qb * n_heads + h] = carries[h]
            live = jnp.maximum(live, jnp.where(more, carries[h], neg_inf))

    def tail_cond(state):
        return state[1] > exit_floor

    def tail_body(state):
        n = state[0]
        live = neg_inf
        for qb in range(nqb):
            j = first_qblock + qb - ATTN_STATIC_BLOCKS - n
            more = j >= 1
            pv, carries = sweep(qb, [j], [car_ref[qb * n_heads + h] for h in range(n_heads)], False)
            for h in range(n_heads):
                acc_ref[qb, head_rows(h), :] += pv[h]
                car_ref[qb * n_heads + h] = carries[h]
                live = jnp.maximum(live, jnp.where(more, carries[h], neg_inf))
        return n + 1, jnp.max(live)

    lax.while_loop(tail_cond, tail_body, (jnp.int32(0), jnp.max(live)))

    for qb in range(nqb):
        rows = slice(qb * BLK, (qb + 1) * BLK)
        for p in range(width // LANES):
            lanes = slice(p * LANES, (p + 1) * LANES)
            o_ref[0, rows, lanes] = (acc_ref[qb, lanes, :].T * sb_ref[0, rows, lanes]).astype(_BF16)


def _sb_attention(q_t, k_blocks, v_t, sb3, *, exit_floor=ATTN_LOG_KEEP_FLOOR):
    batch, nblk, width, _ = q_t.shape
    nqb = ATTN_QBLOCKS
    tq = nqb * BLK
    n_heads = width // HEAD_DIM
    resident = lambda a: pl.BlockSpec((1,) + a.shape[1:], lambda b, i: (b, 0, 0, 0))
    return pl.pallas_call(
        functools.partial(_attn_kernel, exit_floor=exit_floor),
        grid=(batch, nblk // nqb),
        in_specs=[pl.BlockSpec((1, nqb, width, BLK), lambda b, i: (b, i, 0, 0)),
                  resident(k_blocks), resident(v_t),
                  pl.BlockSpec((1, tq, width), lambda b, i: (b, i, 0))],
        out_specs=pl.BlockSpec((1, tq, width), lambda b, i: (b, i, 0)),
        out_shape=jax.ShapeDtypeStruct(sb3.shape, _BF16),
        scratch_shapes=[pltpu.VMEM((nqb, width, BLK), _F32),
                        pltpu.VMEM((nqb * n_heads, ROWS_F32, LANES), _F32)],
        compiler_params=pltpu.CompilerParams(
            dimension_semantics=("arbitrary", "arbitrary"),
            vmem_limit_bytes=VMEM_LIMIT_BYTES),
        name="sb_attn",
    )(q_t, k_blocks, v_t, sb3)


def _outproj_kernel(oa_ref, ob_ref, x_ref, mod_ref, w_ref, g_ref, b_ref, o_ref, *, alpha):
    width = oa_ref.shape[1]
    d_model = x_ref.shape[1]
    y = (jnp.dot(oa_ref[...], w_ref[:width, :], preferred_element_type=_F32)
         + jnp.dot(ob_ref[...], w_ref[width:, :], preferred_element_type=_F32))
    gate = mod_ref[0][:, 2 * d_model:]
    r = alpha * x_ref[...] + (1.0 + gate) * y
    o_ref[...] = _layer_norm_rows(r, g_ref[...], b_ref[...])


def _out_proj(out_a, out_b, x2, mod3, w_out, ln_g, ln_b, *, seq, alpha):
    m, d_model = x2.shape
    width = out_a.shape[1]
    tm = PROJ_ROWS
    steps_per_batch = seq // tm
    row_spec = lambda cols: pl.BlockSpec((tm, cols), lambda i: (i, 0))
    full = lambda a: pl.BlockSpec(a.shape, lambda i: (0,) * a.ndim)
    return pl.pallas_call(
        functools.partial(_outproj_kernel, alpha=alpha),
        grid=(m // tm,),
        in_specs=[row_spec(width), row_spec(width), row_spec(d_model),
                  pl.BlockSpec((1, 1, mod3.shape[-1]), lambda i: (i // steps_per_batch, 0, 0)),
                  full(w_out), full(ln_g), full(ln_b)],
        out_specs=row_spec(d_model),
        out_shape=jax.ShapeDtypeStruct((m, d_model), _F32),
        compiler_params=pltpu.CompilerParams(vmem_limit_bytes=VMEM_LIMIT_BYTES),
        name="out_proj",
    )(out_a, out_b, x2, mod3, w_out, ln_g, ln_b)


def kernel(x, c, w_ada, b_ada, w_in, ln_v_g, ln_v_b, w_spatial, b_spatial, w_out, ln_g, ln_b):
    batch, seq, d_model = x.shape
    depth = w_ada.shape[0]
    width = ln_v_g.shape[-1]
    n_heads = w_spatial.shape[1]
    assert width == n_heads * HEAD_DIM and w_in.shape[-1] == 7 * width
    assert w_spatial.shape[2:] == (BLK, BLK) and seq % PROJ_ROWS == 0
    assert (seq // BLK) % ATTN_QBLOCKS == 0 and PROJ_ROWS % BLK == 0
    alpha = (2 * depth) ** 0.25

    c_pad = jnp.zeros((ROWS_BF16, d_model), x.dtype).at[:batch].set(c)
    x2 = x.reshape(batch * seq, d_model)
    for l in range(depth):
        grp = lambda g: w_in[l][:, g * width:(g + 1) * width]
        w_main = jnp.concatenate([grp(0), grp(1), grp(2), grp(4), grp(6)], axis=1).astype(_BF16)
        w_qv_t = jnp.concatenate([grp(3), grp(5)], axis=1).T.astype(_BF16)
        bias_tile = jnp.repeat(b_spatial[l].T, HEAD_DIM, axis=1)

        mod = _adaln_mod(c_pad, w_ada[l], b_ada[l][None, :])[:batch]
        mod3 = mod[:, None, :]
        out_a, q_t, k, v_t, sb = _in_proj(
            x2, mod3, w_main, w_qv_t, ln_v_g[l][None, :], ln_v_b[l][None, :], w_spatial[l],
            bias_tile, batch=batch, seq=seq)
        out_b = _sb_attention(q_t, k.reshape(batch, seq // BLK, BLK, width), v_t,
                              sb.reshape(batch, seq, width))
        x2 = _out_proj(out_a, out_b.reshape(batch * seq, width), x2, mod3,
                       w_out[l].astype(_BF16), ln_g[l][None, :], ln_b[l][None, :],
                       seq=seq, alpha=alpha)
    return x2.reshape(batch, seq, d_model)
```

```python
import functools
import math

import jax
import jax.numpy as jnp
from jax import lax
from jax.experimental import pallas as pl
from jax.experimental.pallas import tpu as pltpu

HEAD_DIM = 64
CHUNK = 64
BLK = 128
LN_EPS = 1e-5

LANES = 128
ROWS_F32 = 8
VMEM_LIMIT_BYTES = 56 * 1024 * 1024

IN_PROJ_ROWS = 1024
OUT_PROJ_ROWS = 1024
OUT_PROJ_SUB_ROWS = 256
ADALN_COLS = 512
ATTN_QBLOCKS = 4
ATTN_STATIC_HALF_BLOCKS = 5
ATTN_SKEW = 4
ATTN_LOG_KEEP_FLOOR = -110.0

_F32 = jnp.float32
_BF16 = jnp.bfloat16
_NT = (((1,), (1,)), ((), ()))
_LOG2E = 1.4426950408889634
_MASKED_SCORE = -1e30


def _layer_norm_rows(v, g, b):
    mu = jnp.mean(v, axis=-1, keepdims=True)
    d = v - mu
    var = jnp.mean(d * d, axis=-1, keepdims=True)
    return d * lax.rsqrt(var + LN_EPS) * g + b


def _adaln_kernel(c_ref, w_ref, b_ref, o_ref):
    c_rows = jnp.concatenate([jnp.broadcast_to(c_ref[i:i + 1, :], (ROWS_F32, c_ref.shape[1]))
                              for i in range(c_ref.shape[0])], axis=0)
    o_ref[...] = jnp.dot(c_rows.astype(_BF16), w_ref[...].astype(_BF16),
                         preferred_element_type=_F32) + b_ref[...]


def _adaln_mod(c, w_ada, b_ada):
    batch, d = c.shape
    n = w_ada.shape[1]
    tn = ADALN_COLS
    return pl.pallas_call(
        _adaln_kernel,
        grid=(n // tn,),
        in_specs=[pl.BlockSpec((batch, d), lambda j: (0, 0)),
                  pl.BlockSpec((d, tn), lambda j: (0, j)),
                  pl.BlockSpec((1, tn), lambda j: (0, j))],
        out_specs=pl.BlockSpec((batch * ROWS_F32, tn), lambda j: (0, j)),
        out_shape=jax.ShapeDtypeStruct((batch * ROWS_F32, n), _F32),
        compiler_params=pltpu.CompilerParams(vmem_limit_bytes=VMEM_LIMIT_BYTES),
        name="adaln_mod",
    )(c, w_ada, b_ada)


def _mod_row(mod_ref, step, steps_per_batch):
    first = pl.multiple_of((step // steps_per_batch) * ROWS_F32, ROWS_F32)
    return mod_ref[pl.ds(first, ROWS_F32), :][0:1, :]


def _wprep_kernel(w_ref, wb_ref, wt_ref, *, transposed_groups):
    g = pl.program_id(0)
    wb_ref[...] = w_ref[...].astype(_BF16)
    for t in transposed_groups:
        @pl.when(g == t)
        def _():
            wt_ref[...] = w_ref[...].T.astype(_BF16)


def _prep_in_weights(w_in, width, transposed_groups):
    d_model, n = w_in.shape
    q_group, v_group = transposed_groups
    assert q_group < v_group
    return pl.pallas_call(
        functools.partial(_wprep_kernel, transposed_groups=transposed_groups),
        grid=(n // width,),
        in_specs=[pl.BlockSpec((d_model, width), lambda g: (0, g))],
        out_specs=[pl.BlockSpec((d_model, width), lambda g: (0, g)),
                   pl.BlockSpec((width, d_model), lambda g: (jnp.where(g > q_group, 1, 0), 0))],
        out_shape=[jax.ShapeDtypeStruct((d_model, n), _BF16),
                   jax.ShapeDtypeStruct((2 * width, d_model), _BF16)],
        compiler_params=pltpu.CompilerParams(dimension_semantics=("arbitrary",),
                                             vmem_limit_bytes=VMEM_LIMIT_BYTES),
        name="w_in_prep",
    )(w_in)


_G_U, _G_V, _G_ZA, _G_Q, _G_K, _G_VB, _G_ZB = range(7)


def _inproj_kernel(x_ref, mod_ref, w_ref, wt_ref, lng_ref, lnb_ref, ws_ref, bs_ref,
                   oa_ref, qt_ref, k_ref, vt_ref, sb_ref, *, d_model, width, steps_per_batch):
    tm = x_ref.shape[0]
    mod = _mod_row(mod_ref, pl.program_id(0), steps_per_batch)
    shift = mod[:, :d_model]
    scale = mod[:, d_model:2 * d_model]
    h = (x_ref[...] * (1.0 + scale) + shift).astype(_BF16)

    def group(g):
        return jnp.dot(h, w_ref[:, g * width:(g + 1) * width], preferred_element_type=_F32)

    u = jax.nn.gelu(group(_G_U))
    vn = _layer_norm_rows(jax.nn.gelu(group(_G_V)), lng_ref[...], lnb_ref[...]).astype(_BF16)
    za = group(_G_ZA)
    gate_a = u * (za * jax.nn.sigmoid(za))

    row = lax.broadcasted_iota(jnp.int32, (BLK, BLK), 0)
    col = lax.broadcasted_iota(jnp.int32, (BLK, BLK), 1)
    causal = (col // CHUNK) <= (row // CHUNK)
    lane = lax.broadcasted_iota(jnp.int32, (BLK, LANES), 1)
    first_head = lane < HEAD_DIM
    heads_per_lane_group = LANES // HEAD_DIM
    blocks_per_dot = 2
    for p in range(width // LANES):
        w_pair = jnp.concatenate(
            [jnp.where(causal, ws_ref[heads_per_lane_group * p + i], 0.0).astype(_BF16)
             for i in range(heads_per_lane_group)], axis=0)
        cols = slice(p * LANES, (p + 1) * LANES)
        for r0 in range(0, tm // BLK, blocks_per_dot):
            blocks = [slice(r * BLK, (r + 1) * BLK) for r in range(r0, r0 + blocks_per_dot)]
            res = jnp.dot(w_pair, jnp.concatenate([vn[rows, cols] for rows in blocks], axis=1),
                          preferred_element_type=_F32)
            for i, rows in enumerate(blocks):
                one = res[:, i * LANES:(i + 1) * LANES]
                mixed = jnp.where(first_head, one[:BLK], one[BLK:]) + bs_ref[:, cols]
                oa_ref[rows, cols] = (gate_a[rows, cols] * mixed).astype(_BF16)

    k_ref[...] = group(_G_K).astype(_BF16)
    zb = group(_G_ZB)
    sb_ref[...] = zb * jax.nn.sigmoid(zb)
    qv_t = lax.dot_general(wt_ref[...], h, _NT, preferred_element_type=_F32)
    q_scale = 1.0 / math.sqrt(HEAD_DIM)
    for r in range(tm // BLK):
        cols = slice(r * BLK, (r + 1) * BLK)
        qt_ref[0, r] = (qv_t[:width, cols] * q_scale).astype(_BF16)
        vt_ref[0, r] = qv_t[width:, cols].astype(_BF16)


def _in_proj(x2, mod, w_bf16, w_qv_t, ln_v_g, ln_v_b, w_spatial, bias_tile, *, batch, seq):
    m, d_model = x2.shape
    width = ln_v_g.shape[-1]
    tm = IN_PROJ_ROWS
    steps_per_batch = seq // tm
    nblk = seq // BLK
    row_spec = lambda cols: pl.BlockSpec((tm, cols), lambda i: (i, 0))
    full = lambda a: pl.BlockSpec(a.shape, lambda i: (0,) * a.ndim)
    t_spec = pl.BlockSpec((1, tm // BLK, width, BLK),
                          lambda i: (i // steps_per_batch, i % steps_per_batch, 0, 0))
    return pl.pallas_call(
        functools.partial(_inproj_kernel, d_model=d_model, width=width,
                          steps_per_batch=steps_per_batch),
        grid=(m // tm,),
        in_specs=[row_spec(d_model), full(mod), full(w_bf16), full(w_qv_t), full(ln_v_g),
                  full(ln_v_b), full(w_spatial), full(bias_tile)],
        out_specs=[row_spec(width), t_spec, row_spec(width), t_spec, row_spec(width)],
        out_shape=[jax.ShapeDtypeStruct((m, width), _BF16),
                   jax.ShapeDtypeStruct((batch, nblk, width, BLK), _BF16),
                   jax.ShapeDtypeStruct((m, width), _BF16),
                   jax.ShapeDtypeStruct((batch, nblk, width, BLK), _BF16),
                   jax.ShapeDtypeStruct((m, width), _F32)],
        compiler_params=pltpu.CompilerParams(vmem_limit_bytes=VMEM_LIMIT_BYTES),
        name="in_proj",
    )(x2, mod, w_bf16, w_qv_t, ln_v_g, ln_v_b, w_spatial, bias_tile)


def _sb_softplus_split(z):
    softplus = jnp.maximum(z, 0.0) + jnp.log(1.0 + jnp.exp2(jnp.abs(z) * (-_LOG2E)))
    hi = softplus.astype(_BF16)
    lo = (softplus - hi.astype(_F32)).astype(_BF16)
    return jnp.concatenate([hi, lo], axis=0)


def _attn_kernel(qt_ref, k_ref, vt_ref, sb_ref, o_ref, acc_ref, dec_ref, *, exit_floor):
    nqb = qt_ref.shape[1]
    width = qt_ref.shape[2]
    n_groups = width // LANES
    pair = 2 * LANES
    half = BLK // 2
    first_qblock = pl.program_id(1) * nqb

    def cum_matrix(nk):
        row = lax.broadcasted_iota(jnp.int32, (nk, nk), 0)
        col = lax.broadcasted_iota(jnp.int32, (nk, nk), 1)
        m = jnp.where(col >= row, 1.0, 0.0).astype(_BF16)
        return jnp.concatenate([m, m], axis=1)

    cum = {BLK: cum_matrix(BLK), half: cum_matrix(half)}
    key_before_query = (lax.broadcasted_iota(jnp.int32, (BLK, pair), 0)
                        < (lax.broadcasted_iota(jnp.int32, (BLK, pair), 1) & (LANES - 1)))
    zero_rows = jnp.zeros((HEAD_DIM, LANES), _BF16)
    zero_half = jnp.zeros((half, pair), _BF16)
    lanes_of = lambda g: slice(g * LANES, (g + 1) * LANES)

    def q_block_diag(qb, g):
        q = qt_ref[0, qb, lanes_of(g), :]
        left = jnp.concatenate([q[:HEAD_DIM], zero_rows], axis=0)
        right = jnp.concatenate([zero_rows, q[HEAD_DIM:]], axis=0)
        return jnp.concatenate([left, right], axis=1)

    def sweep(jobs):
        chains, q_bd, dec, acc = [], {}, {}, {}
        for qb, spans, decay in jobs:
            for g in range(n_groups):
                q_bd[qb, g] = q_block_diag(qb, g)
                dec[qb, g] = None if decay is None else decay[g]
                acc[qb, g] = None
                chains.extend((qb, g) + span for span in spans)
        z, split, w, scale = {}, {}, {}, {}

        def scores(c):
            qb, g, jc, row0, nk, present, diagonal = chains[c]
            rows = slice(row0, row0 + nk) if isinstance(row0, int) else pl.ds(row0, nk)
            zc = jnp.dot(k_ref[0, jc, rows, lanes_of(g)], q_bd[qb, g], preferred_element_type=_F32)
            if diagonal:
                zc = jnp.where(key_before_query, zc, _MASKED_SCORE)
            z[c] = zc
            split[c] = _sb_softplus_split(zc)

        def weights(c):
            qb, g, jc, row0, nk, present, diagonal = chains[c]
            cs = jnp.dot(cum[nk], split.pop(c), preferred_element_type=_F32)
            w[c] = jnp.exp(z.pop(c) - cs).astype(_BF16)
            total = jnp.broadcast_to(cs[0:1, :], (ROWS_F32, pair))
            if present is not None:
                total = jnp.where(present, total, 0.0)
            scale[c] = None if dec[qb, g] is None else jnp.exp(-dec[qb, g])
            dec[qb, g] = total if dec[qb, g] is None else dec[qb, g] + total

        def values(c):
            qb, g, jc, row0, nk, present, diagonal = chains[c]
            wc = w.pop(c)
            if nk == half:
                if isinstance(row0, int):
                    wc = jnp.concatenate([zero_half, wc] if row0 else [wc, zero_half], axis=0)
                else:
                    wc = jnp.where(row0 > 0, jnp.concatenate([zero_half, wc], axis=0),
                                   jnp.concatenate([wc, zero_half], axis=0))
            both = jnp.dot(vt_ref[0, jc, lanes_of(g), :], wc, preferred_element_type=_F32)
            first, second = both[:HEAD_DIM, :LANES], both[HEAD_DIM:, LANES:]
            sc = scale.pop(c)
            if sc is not None:
                first, second = first * sc[0:1, :LANES], second * sc[0:1, LANES:]
            pv = jnp.concatenate([first, second], axis=0)
            if present is not None:
                pv = jnp.where(present, pv, 0.0)
            acc[qb, g] = pv if acc[qb, g] is None else acc[qb, g] + pv

        stages = (scores, weights, values)
        for step in range(len(chains) + ATTN_SKEW * (len(stages) - 1)):
            for s, stage in enumerate(stages):
                c = step - ATTN_SKEW * s
                if 0 <= c < len(chains):
                    stage(c)
        return [([acc[qb, g] for g in range(n_groups)], [dec[qb, g] for g in range(n_groups)])
                for qb, _, _ in jobs]

    pos_inf = jnp.full((ROWS_F32, pair), jnp.inf, _F32)
    q_index = lambda qb: first_qblock + qb

    def static_spans(i_q):
        spans = [(i_q, 0, BLK, None, True)]
        for d in range(1, ATTN_STATIC_HALF_BLOCKS // 2):
            spans.append((jnp.maximum(i_q - d, 0), 0, BLK, i_q - d >= 0, False))
        d = ATTN_STATIC_HALF_BLOCKS // 2
        spans.append((jnp.maximum(i_q - d, 0), half, half, i_q - d >= 0, False))
        return spans

    out = sweep([(qb, static_spans(q_index(qb)), None) for qb in range(nqb)])

    def commit(out, more, accumulate):
        least = pos_inf
        for qb, (pv, decay) in enumerate(out):
            for g in range(n_groups):
                if accumulate:
                    acc_ref[qb, lanes_of(g), :] += pv[g]
                else:
                    acc_ref[qb, lanes_of(g), :] = pv[g]
                dec_ref[qb * n_groups + g] = decay[g]
                least = jnp.minimum(least, jnp.where(more[qb], decay[g], pos_inf))
        return jnp.min(least)

    next_half = lambda qb, n: 2 * q_index(qb) - (ATTN_STATIC_HALF_BLOCKS - 1) - n
    least = commit(out, [next_half(qb, 0) >= 0 for qb in range(nqb)], False)

    def tail_cond(state):
        return -state[1] > exit_floor

    def tail_body(state):
        n = state[0]
        jobs = []
        for qb in range(nqb):
            hb = next_half(qb, n)
            hbc = jnp.maximum(hb, 0)
            row0 = pl.multiple_of((hbc & 1) * half, half)
            jobs.append((qb, [(hbc >> 1, row0, half, hb >= 0, False)],
                         [dec_ref[qb * n_groups + g] for g in range(n_groups)]))
        return n + 1, commit(sweep(jobs), [next_half(qb, n) >= 1 for qb in range(nqb)], True)

    lax.while_loop(tail_cond, tail_body, (jnp.int32(0), least))

    for qb in range(nqb):
        rows = slice(qb * BLK, (qb + 1) * BLK)
        for g in range(n_groups):
            o_ref[0, rows, lanes_of(g)] = (acc_ref[qb, lanes_of(g), :].T
                                           * sb_ref[0, rows, lanes_of(g)]).astype(_BF16)


def _sb_attention(q_t, k_blocks, v_t, sb3, *, exit_floor=ATTN_LOG_KEEP_FLOOR):
    batch, nblk, width, _ = q_t.shape
    nqb = ATTN_QBLOCKS
    tq = nqb * BLK
    resident = lambda a: pl.BlockSpec((1,) + a.shape[1:], lambda b, i: (b, 0, 0, 0))
    return pl.pallas_call(
        functools.partial(_attn_kernel, exit_floor=exit_floor),
        grid=(batch, nblk // nqb),
        in_specs=[pl.BlockSpec((1, nqb, width, BLK), lambda b, i: (b, i, 0, 0)),
                  resident(k_blocks), resident(v_t),
                  pl.BlockSpec((1, tq, width), lambda b, i: (b, i, 0))],
        out_specs=pl.BlockSpec((1, tq, width), lambda b, i: (b, i, 0)),
        out_shape=jax.ShapeDtypeStruct(sb3.shape, _BF16),
        scratch_shapes=[pltpu.VMEM((nqb, width, BLK), _F32),
                        pltpu.VMEM((nqb * width // LANES, ROWS_F32, 2 * LANES), _F32)],
        compiler_params=pltpu.CompilerParams(
            dimension_semantics=("arbitrary", "arbitrary"),
            vmem_limit_bytes=VMEM_LIMIT_BYTES),
        name="sb_attn",
    )(q_t, k_blocks, v_t, sb3)


def _outproj_kernel(oa_ref, ob_ref, x_ref, mod_ref, w_ref, g_ref, b_ref, o_ref, wb_ref, *,
                    alpha, steps_per_batch):
    width = oa_ref.shape[1]
    d_model = x_ref.shape[1]

    @pl.when(pl.program_id(0) == 0)
    def _():
        wb_ref[...] = w_ref[...].astype(_BF16)

    gate = 1.0 + _mod_row(mod_ref, pl.program_id(0), steps_per_batch)[:, 2 * d_model:]
    for r0 in range(0, x_ref.shape[0], OUT_PROJ_SUB_ROWS):
        rows = slice(r0, r0 + OUT_PROJ_SUB_ROWS)
        y = (jnp.dot(oa_ref[rows, :], wb_ref[:width, :], preferred_element_type=_F32)
             + jnp.dot(ob_ref[rows, :], wb_ref[width:, :], preferred_element_type=_F32))
        r = alpha * x_ref[rows, :] + gate * y
        o_ref[rows, :] = _layer_norm_rows(r, g_ref[...], b_ref[...])


def _out_proj(out_a, out_b, x2, mod, w_out, ln_g, ln_b, *, seq, alpha):
    m, d_model = x2.shape
    width = out_a.shape[1]
    tm = OUT_PROJ_ROWS
    steps_per_batch = seq // tm
    row_spec = lambda cols: pl.BlockSpec((tm, cols), lambda i: (i, 0))
    full = lambda a: pl.BlockSpec(a.shape, lambda i: (0,) * a.ndim)
    return pl.pallas_call(
        functools.partial(_outproj_kernel, alpha=alpha, steps_per_batch=steps_per_batch),
        grid=(m // tm,),
        in_specs=[row_spec(width), row_spec(width), row_spec(d_model), full(mod), full(w_out),
                  full(ln_g), full(ln_b)],
        out_specs=row_spec(d_model),
        out_shape=jax.ShapeDtypeStruct((m, d_model), _F32),
        scratch_shapes=[pltpu.VMEM(w_out.shape, _BF16)],
        compiler_params=pltpu.CompilerParams(dimension_semantics=("arbitrary",),
                                             vmem_limit_bytes=VMEM_LIMIT_BYTES),
        name="out_proj",
    )(out_a, out_b, x2, mod, w_out, ln_g, ln_b)


def kernel(x, c, w_ada, b_ada, w_in, ln_v_g, ln_v_b, w_spatial, b_spatial, w_out, ln_g, ln_b):
    batch, seq, d_model = x.shape
    depth = w_ada.shape[0]
    width = ln_v_g.shape[-1]
    n_heads = w_spatial.shape[1]
    assert width == n_heads * HEAD_DIM and w_in.shape[-1] == 7 * width
    assert w_spatial.shape[2:] == (BLK, BLK)
    assert seq % IN_PROJ_ROWS == 0 and seq % OUT_PROJ_ROWS == 0 and IN_PROJ_ROWS % (2 * BLK) == 0
    assert (seq // BLK) % ATTN_QBLOCKS == 0 and (3 * d_model) % ADALN_COLS == 0
    alpha = (2 * depth) ** 0.25

    x2 = x.reshape(batch * seq, d_model)
    for l in range(depth):
        w_bf16, w_qv_t = _prep_in_weights(w_in[l], width, (_G_Q, _G_VB))
        bias_tile = jnp.repeat(b_spatial[l].T, HEAD_DIM, axis=1)
        mod = _adaln_mod(c, w_ada[l], b_ada[l][None, :])
        out_a, q_t, k, v_t, sb = _in_proj(
            x2, mod, w_bf16, w_qv_t, ln_v_g[l][None, :], ln_v_b[l][None, :], w_spatial[l],
            bias_tile, batch=batch, seq=seq)
        out_b = _sb_attention(q_t, k.reshape(batch, seq // BLK, BLK, width), v_t,
                              sb.reshape(batch, seq, width))
        x2 = _out_proj(out_a, out_b.reshape(batch * seq, width), x2, mod, w_out[l],
                       ln_g[l][None, :], ln_b[l][None, :], seq=seq, alpha=alpha)
    return x2.reshape(batch, seq, d_model)
```

```python
import functools
import math

import jax
import jax.numpy as jnp
from jax import lax
from jax.experimental import pallas as pl
from jax.experimental.pallas import tpu as pltpu

HEAD_DIM = 64
CHUNK = 64
BLK = 128
LN_EPS = 1e-5

LANES = 128
ROWS_F32 = 8
VMEM_LIMIT_BYTES = 56 * 1024 * 1024

IN_PROJ_ROWS = 1024
IN_PROJ_SUB_ROWS = 512
OUT_PROJ_ROWS = 1024
OUT_PROJ_SUB_ROWS = 256
ADALN_COLS = 512
ATTN_QBLOCKS = 8
ATTN_STATIC_HALF_BLOCKS = 5
ATTN_SKEW = 4
ATTN_LOG_KEEP_FLOOR = -110.0

_F32 = jnp.float32
_BF16 = jnp.bfloat16
_NT = (((1,), (1,)), ((), ()))
_LOG2E = 1.4426950408889634
_MASKED_SCORE = -1e30


def _layer_norm_rows(v, g, b):
    mu = jnp.mean(v, axis=-1, keepdims=True)
    d = v - mu
    var = jnp.mean(d * d, axis=-1, keepdims=True)
    return d * lax.rsqrt(var + LN_EPS) * g + b


def _adaln_kernel(c_ref, w_ref, b_ref, o_ref):
    c_rows = jnp.concatenate([jnp.broadcast_to(c_ref[i:i + 1, :], (ROWS_F32, c_ref.shape[1]))
                              for i in range(c_ref.shape[0])], axis=0)
    o_ref[...] = jnp.dot(c_rows.astype(_BF16), w_ref[...].astype(_BF16),
                         preferred_element_type=_F32) + b_ref[...]


def _adaln_mod(c, w_ada, b_ada):
    batch, d = c.shape
    n = w_ada.shape[1]
    tn = ADALN_COLS
    return pl.pallas_call(
        _adaln_kernel,
        grid=(n // tn,),
        in_specs=[pl.BlockSpec((batch, d), lambda j: (0, 0)),
                  pl.BlockSpec((d, tn), lambda j: (0, j)),
                  pl.BlockSpec((1, tn), lambda j: (0, j))],
        out_specs=pl.BlockSpec((batch * ROWS_F32, tn), lambda j: (0, j)),
        out_shape=jax.ShapeDtypeStruct((batch * ROWS_F32, n), _F32),
        compiler_params=pltpu.CompilerParams(vmem_limit_bytes=VMEM_LIMIT_BYTES),
        name="adaln_mod",
    )(c, w_ada, b_ada)


def _mod_row(mod_ref, step, steps_per_batch):
    first = pl.multiple_of((step // steps_per_batch) * ROWS_F32, ROWS_F32)
    return mod_ref[pl.ds(first, ROWS_F32), :][0:1, :]


def _wprep_kernel(w_ref, wb_ref, wt_ref, *, transposed_groups):
    g = pl.program_id(0)
    wb_ref[...] = w_ref[...].astype(_BF16)
    for t in transposed_groups:
        @pl.when(g == t)
        def _():
            wt_ref[...] = w_ref[...].T.astype(_BF16)


def _prep_in_weights(w_in, width, transposed_groups):
    d_model, n = w_in.shape
    q_group, v_group = transposed_groups
    assert q_group < v_group
    return pl.pallas_call(
        functools.partial(_wprep_kernel, transposed_groups=transposed_groups),
        grid=(n // width,),
        in_specs=[pl.BlockSpec((d_model, width), lambda g: (0, g))],
        out_specs=[pl.BlockSpec((d_model, width), lambda g: (0, g)),
                   pl.BlockSpec((width, d_model), lambda g: (jnp.where(g > q_group, 1, 0), 0))],
        out_shape=[jax.ShapeDtypeStruct((d_model, n), _BF16),
                   jax.ShapeDtypeStruct((2 * width, d_model), _BF16)],
        compiler_params=pltpu.CompilerParams(dimension_semantics=("arbitrary",),
                                             vmem_limit_bytes=VMEM_LIMIT_BYTES),
        name="w_in_prep",
    )(w_in)


_G_U, _G_V, _G_ZA, _G_Q, _G_K, _G_VB, _G_ZB = range(7)


def _inproj_kernel(x_ref, mod_ref, w_ref, wt_ref, lng_ref, lnb_ref, ws_ref, bs_ref,
                   oa_ref, qt_ref, k_ref, vt_ref, sb_ref, *, d_model, width, steps_per_batch):
    mod = _mod_row(mod_ref, pl.program_id(0), steps_per_batch)
    shift = mod[:, :d_model]
    scale = 1.0 + mod[:, d_model:2 * d_model]

    row = lax.broadcasted_iota(jnp.int32, (BLK, BLK), 0)
    col = lax.broadcasted_iota(jnp.int32, (BLK, BLK), 1)
    causal = (col // CHUNK) <= (row // CHUNK)
    heads_per_lane_group = LANES // HEAD_DIM
    w_pairs = [jnp.concatenate(
        [jnp.where(causal, ws_ref[heads_per_lane_group * p + i], 0.0).astype(_BF16)
         for i in range(heads_per_lane_group)], axis=1) for p in range(width // LANES)]
    head_of_lane = lax.broadcasted_iota(jnp.int32, (BLK, LANES), 1) // HEAD_DIM
    zero_tile = jnp.zeros((BLK, LANES), _BF16)
    blocks_per_dot = 2
    q_scale = 1.0 / math.sqrt(HEAD_DIM)

    for s0 in range(0, x_ref.shape[0], IN_PROJ_SUB_ROWS):
        sub = slice(s0, s0 + IN_PROJ_SUB_ROWS)
        h = (x_ref[sub, :] * scale + shift).astype(_BF16)

        def group(g):
            return jnp.dot(h, w_ref[:, g * width:(g + 1) * width], preferred_element_type=_F32)

        u = jax.nn.gelu(group(_G_U))
        vn = _layer_norm_rows(jax.nn.gelu(group(_G_V)), lng_ref[...], lnb_ref[...]).astype(_BF16)
        za = group(_G_ZA)
        gate_a = u * (za * jax.nn.sigmoid(za))
        for p in range(width // LANES):
            cols = slice(p * LANES, (p + 1) * LANES)
            for r0 in range(0, IN_PROJ_SUB_ROWS // BLK, blocks_per_dot):
                blocks = [slice(r * BLK, (r + 1) * BLK) for r in range(r0, r0 + blocks_per_dot)]
                stacked = jnp.concatenate(
                    [jnp.concatenate([jnp.where(head_of_lane == i, vn[rows, cols], zero_tile)
                                      for i in range(heads_per_lane_group)], axis=0)
                     for rows in blocks], axis=1)
                res = jnp.dot(w_pairs[p], stacked, preferred_element_type=_F32)
                for i, rows in enumerate(blocks):
                    mixed = res[:, i * LANES:(i + 1) * LANES] + bs_ref[:, cols]
                    oa_ref[s0 + rows.start:s0 + rows.stop, cols] = (gate_a[rows, cols] * mixed).astype(_BF16)

        k_ref[sub, :] = group(_G_K).astype(_BF16)
        zb = group(_G_ZB)
        sb_ref[sub, :] = zb * jax.nn.sigmoid(zb)
        qv_t = lax.dot_general(wt_ref[...], h, _NT, preferred_element_type=_F32)
        for r in range(IN_PROJ_SUB_ROWS // BLK):
            cols = slice(r * BLK, (r + 1) * BLK)
            qt_ref[0, s0 // BLK + r] = (qv_t[:width, cols] * q_scale).astype(_BF16)
            vt_ref[0, s0 // BLK + r] = qv_t[width:, cols].astype(_BF16)


def _in_proj(x2, mod, w_bf16, w_qv_t, ln_v_g, ln_v_b, w_spatial, bias_tile, *, batch, seq):
    m, d_model = x2.shape
    width = ln_v_g.shape[-1]
    tm = IN_PROJ_ROWS
    steps_per_batch = seq // tm
    nblk = seq // BLK
    row_spec = lambda cols: pl.BlockSpec((tm, cols), lambda i: (i, 0))
    full = lambda a: pl.BlockSpec(a.shape, lambda i: (0,) * a.ndim)
    t_spec = pl.BlockSpec((1, tm // BLK, width, BLK),
                          lambda i: (i // steps_per_batch, i % steps_per_batch, 0, 0))
    return pl.pallas_call(
        functools.partial(_inproj_kernel, d_model=d_model, width=width,
                          steps_per_batch=steps_per_batch),
        grid=(m // tm,),
        in_specs=[row_spec(d_model), full(mod), full(w_bf16), full(w_qv_t), full(ln_v_g),
                  full(ln_v_b), full(w_spatial), full(bias_tile)],
        out_specs=[row_spec(width), t_spec, row_spec(width), t_spec, row_spec(width)],
        out_shape=[jax.ShapeDtypeStruct((m, width), _BF16),
                   jax.ShapeDtypeStruct((batch, nblk, width, BLK), _BF16),
                   jax.ShapeDtypeStruct((m, width), _BF16),
                   jax.ShapeDtypeStruct((batch, nblk, width, BLK), _BF16),
                   jax.ShapeDtypeStruct((m, width), _F32)],
        compiler_params=pltpu.CompilerParams(vmem_limit_bytes=VMEM_LIMIT_BYTES),
        name="in_proj",
    )(x2, mod, w_bf16, w_qv_t, ln_v_g, ln_v_b, w_spatial, bias_tile)


def _sb_softplus(z):
    return jnp.maximum(z, 0.0) + jnp.log(1.0 + jnp.exp2(jnp.abs(z) * (-_LOG2E)))


def _bf16_split(v):
    hi = v.astype(_BF16)
    lo = (v - hi.astype(_F32)).astype(_BF16)
    return jnp.concatenate([hi, lo], axis=0)


def _attn_kernel(qt_ref, k_ref, vt_ref, sb_ref, o_ref, acc_ref, dec_ref, *, exit_floor):
    nqb = qt_ref.shape[1]
    width = qt_ref.shape[2]
    n_groups = width // LANES
    pair = 2 * LANES
    half = BLK // 2
    first_qblock = pl.program_id(1) * nqb

    def cum_matrix(nk):
        row = lax.broadcasted_iota(jnp.int32, (nk, nk), 0)
        col = lax.broadcasted_iota(jnp.int32, (nk, nk), 1)
        m = jnp.where(col >= row, 1.0, 0.0).astype(_BF16)
        return jnp.concatenate([m, m], axis=1)

    cum = {BLK: cum_matrix(BLK), half: cum_matrix(half)}
    causal_bias = jnp.where(lax.broadcasted_iota(jnp.int32, (BLK, pair), 0)
                            < (lax.broadcasted_iota(jnp.int32, (BLK, pair), 1) & (LANES - 1)),
                            0.0, _MASKED_SCORE)
    last_row = lax.broadcasted_iota(jnp.int32, (ROWS_F32, pair), 0) == ROWS_F32 - 1
    zero_rows = jnp.zeros((HEAD_DIM, LANES), _BF16)
    zero_half = jnp.zeros((half, pair), _BF16)
    lanes_of = lambda g: slice(g * LANES, (g + 1) * LANES)

    def q_block_diag(qb, g):
        q = qt_ref[0, qb, lanes_of(g), :]
        left = jnp.concatenate([q[:HEAD_DIM], zero_rows], axis=0)
        right = jnp.concatenate([zero_rows, q[HEAD_DIM:]], axis=0)
        return jnp.concatenate([left, right], axis=1)

    def sweep(jobs):
        chains, q_bd, dec, acc = [], {}, {}, {}
        for qb, spans, decay in jobs:
            for g in range(n_groups):
                q_bd[qb, g] = q_block_diag(qb, g)
                dec[qb, g] = None if decay is None else decay[g]
        for d in range(max(len(spans) for _, spans, _ in jobs)):
            for qb, spans, _ in jobs:
                if d < len(spans):
                    chains.extend((qb, g) + spans[d] for g in range(n_groups))
        z, w = {}, {}

        def scores(c):
            qb, g, jc, row0, nk, present, diagonal = chains[c]
            rows = slice(row0, row0 + nk) if isinstance(row0, int) else pl.ds(row0, nk)
            zc = jnp.dot(k_ref[0, jc, rows, lanes_of(g)], q_bd[qb, g], preferred_element_type=_F32)
            z[c] = zc + causal_bias if diagonal else zc

        def weights(c):
            qb, g, jc, row0, nk, present, diagonal = chains[c]
            sp = _sb_softplus(z[c])
            before = dec[qb, g]
            if before is not None:
                before = before if present is None else jnp.where(present, before, -_MASKED_SCORE)
                sp = jnp.concatenate([sp[:nk - ROWS_F32],
                                      sp[nk - ROWS_F32:] + jnp.where(last_row, before, 0.0)], axis=0)
            cs = jnp.dot(cum[nk], _bf16_split(sp), preferred_element_type=_F32)
            wc = jnp.exp(z.pop(c) - cs).astype(_BF16)
            if nk == half:
                if isinstance(row0, int):
                    wc = jnp.concatenate([zero_half, wc] if row0 else [wc, zero_half], axis=0)
                else:
                    wc = jnp.where(row0 > 0, jnp.concatenate([zero_half, wc], axis=0),
                                   jnp.concatenate([wc, zero_half], axis=0))
            w[c] = wc
            total = jnp.broadcast_to(cs[0:1, :], (ROWS_F32, pair))
            if before is not None:
                total = total - before
            if present is not None:
                total = jnp.where(present, total, 0.0)
            dec[qb, g] = total if dec[qb, g] is None else dec[qb, g] + total

        def values(c):
            qb, g, jc = chains[c][:3]
            both = jnp.dot(vt_ref[0, jc, lanes_of(g), :], w.pop(c), preferred_element_type=_F32)
            pv = jnp.concatenate([both[:HEAD_DIM, :LANES], both[HEAD_DIM:, LANES:]], axis=0)
            acc[qb, g] = pv if acc.get((qb, g)) is None else acc[qb, g] + pv

        stages = (scores, weights, values)
        for step in range(len(chains) + ATTN_SKEW * (len(stages) - 1)):
            for s, stage in enumerate(stages):
                c = step - ATTN_SKEW * s
                if 0 <= c < len(chains):
                    stage(c)
        return [([acc[qb, g] for g in range(n_groups)], [dec[qb, g] for g in range(n_groups)])
                for qb, _, _ in jobs]

    pos_inf = jnp.full((ROWS_F32, pair), jnp.inf, _F32)
    q_index = lambda qb: first_qblock + qb

    def static_spans(i_q):
        spans = [(i_q, 0, BLK, None, True)]
        for d in range(1, ATTN_STATIC_HALF_BLOCKS // 2):
            spans.append((jnp.maximum(i_q - d, 0), 0, BLK, i_q - d >= 0, False))
        d = ATTN_STATIC_HALF_BLOCKS // 2
        spans.append((jnp.maximum(i_q - d, 0), half, half, i_q - d >= 0, False))
        return spans

    out = sweep([(qb, static_spans(q_index(qb)), None) for qb in range(nqb)])

    def commit(out, more, accumulate):
        least = pos_inf
        for qb, (pv, decay) in enumerate(out):
            for g in range(n_groups):
                if accumulate:
                    acc_ref[qb, lanes_of(g), :] += pv[g]
                else:
                    acc_ref[qb, lanes_of(g), :] = pv[g]
                dec_ref[qb * n_groups + g] = decay[g]
                least = jnp.minimum(least, jnp.where(more[qb], decay[g], pos_inf))
        return jnp.min(least)

    next_half = lambda qb, n: 2 * q_index(qb) - (ATTN_STATIC_HALF_BLOCKS - 1) - n
    least = commit(out, [next_half(qb, 0) >= 0 for qb in range(nqb)], False)

    def tail_cond(state):
        return -state[1] > exit_floor

    def tail_body(state):
        n = state[0]
        jobs = []
        for qb in range(nqb):
            hb = next_half(qb, n)
            hbc = jnp.maximum(hb, 0)
            row0 = pl.multiple_of((hbc & 1) * half, half)
            jobs.append((qb, [(hbc >> 1, row0, half, hb >= 0, False)],
                         [dec_ref[qb * n_groups + g] for g in range(n_groups)]))
        return n + 1, commit(sweep(jobs), [next_half(qb, n) >= 1 for qb in range(nqb)], True)

    lax.while_loop(tail_cond, tail_body, (jnp.int32(0), least))

    for qb in range(nqb):
        rows = slice(qb * BLK, (qb + 1) * BLK)
        for g in range(n_groups):
            o_ref[0, rows, lanes_of(g)] = (acc_ref[qb, lanes_of(g), :].T
                                           * sb_ref[0, rows, lanes_of(g)]).astype(_BF16)


def _sb_attention(q_t, k_blocks, v_t, sb3, *, exit_floor=ATTN_LOG_KEEP_FLOOR):
    batch, nblk, width, _ = q_t.shape
    nqb = ATTN_QBLOCKS
    tq = nqb * BLK
    resident = lambda a: pl.BlockSpec((1,) + a.shape[1:], lambda b, i: (b, 0, 0, 0))
    return pl.pallas_call(
        functools.partial(_attn_kernel, exit_floor=exit_floor),
        grid=(batch, nblk // nqb),
        in_specs=[pl.BlockSpec((1, nqb, width, BLK), lambda b, i: (b, i, 0, 0)),
                  resident(k_blocks), resident(v_t),
                  pl.BlockSpec((1, tq, width), lambda b, i: (b, i, 0))],
        out_specs=pl.BlockSpec((1, tq, width), lambda b, i: (b, i, 0)),
        out_shape=jax.ShapeDtypeStruct(sb3.shape, _BF16),
        scratch_shapes=[pltpu.VMEM((nqb, width, BLK), _F32),
                        pltpu.VMEM((nqb * width // LANES, ROWS_F32, 2 * LANES), _F32)],
        compiler_params=pltpu.CompilerParams(
            dimension_semantics=("arbitrary", "arbitrary"),
            vmem_limit_bytes=VMEM_LIMIT_BYTES),
        name="sb_attn",
    )(q_t, k_blocks, v_t, sb3)


def _outproj_kernel(oa_ref, ob_ref, x_ref, mod_ref, w_ref, g_ref, b_ref, o_ref, wb_ref, *,
                    alpha, steps_per_batch):
    width = oa_ref.shape[1]
    d_model = x_ref.shape[1]

    @pl.when(pl.program_id(0) == 0)
    def _():
        wb_ref[...] = w_ref[...].astype(_BF16)

    gate = 1.0 + _mod_row(mod_ref, pl.program_id(0), steps_per_batch)[:, 2 * d_model:]
    for r0 in range(0, x_ref.shape[0], OUT_PROJ_SUB_ROWS):
        rows = slice(r0, r0 + OUT_PROJ_SUB_ROWS)
        y = (jnp.dot(oa_ref[rows, :], wb_ref[:width, :], preferred_element_type=_F32)
             + jnp.dot(ob_ref[rows, :], wb_ref[width:, :], preferred_element_type=_F32))
        r = alpha * x_ref[rows, :] + gate * y
        o_ref[rows, :] = _layer_norm_rows(r, g_ref[...], b_ref[...])


def _out_proj(out_a, out_b, x2, mod, w_out, ln_g, ln_b, *, seq, alpha):
    m, d_model = x2.shape
    width = out_a.shape[1]
    tm = OUT_PROJ_ROWS
    steps_per_batch = seq // tm
    row_spec = lambda cols: pl.BlockSpec((tm, cols), lambda i: (i, 0))
    full = lambda a: pl.BlockSpec(a.shape, lambda i: (0,) * a.ndim)
    return pl.pallas_call(
        functools.partial(_outproj_kernel, alpha=alpha, steps_per_batch=steps_per_batch),
        grid=(m // tm,),
        in_specs=[row_spec(width), row_spec(width), row_spec(d_model), full(mod), full(w_out),
                  full(ln_g), full(ln_b)],
        out_specs=row_spec(d_model),
        out_shape=jax.ShapeDtypeStruct((m, d_model), _F32),
        scratch_shapes=[pltpu.VMEM(w_out.shape, _BF16)],
        compiler_params=pltpu.CompilerParams(dimension_semantics=("arbitrary",),
                                             vmem_limit_bytes=VMEM_LIMIT_BYTES),
        name="out_proj",
    )(out_a, out_b, x2, mod, w_out, ln_g, ln_b)


def kernel(x, c, w_ada, b_ada, w_in, ln_v_g, ln_v_b, w_spatial, b_spatial, w_out, ln_g, ln_b):
    batch, seq, d_model = x.shape
    depth = w_ada.shape[0]
    width = ln_v_g.shape[-1]
    n_heads = w_spatial.shape[1]
    assert width == n_heads * HEAD_DIM and w_in.shape[-1] == 7 * width
    assert w_spatial.shape[2:] == (BLK, BLK)
    assert seq % IN_PROJ_ROWS == 0 and seq % OUT_PROJ_ROWS == 0
    assert IN_PROJ_ROWS % IN_PROJ_SUB_ROWS == 0 and IN_PROJ_SUB_ROWS % (2 * BLK) == 0
    assert (seq // BLK) % ATTN_QBLOCKS == 0 and (3 * d_model) % ADALN_COLS == 0
    alpha = (2 * depth) ** 0.25

    x2 = x.reshape(batch * seq, d_model)
    for l in range(depth):
        w_bf16, w_qv_t = _prep_in_weights(w_in[l], width, (_G_Q, _G_VB))
        bias_tile = jnp.repeat(b_spatial[l].T, HEAD_DIM, axis=1)
        mod = _adaln_mod(c, w_ada[l], b_ada[l][None, :])
        out_a, q_t, k, v_t, sb = _in_proj(
            x2, mod, w_bf16, w_qv_t, ln_v_g[l][None, :], ln_v_b[l][None, :], w_spatial[l],
            bias_tile, batch=batch, seq=seq)
        out_b = _sb_attention(q_t, k.reshape(batch, seq // BLK, BLK, width), v_t,
                              sb.reshape(batch, seq, width))
        x2 = _out_proj(out_a, out_b.reshape(batch * seq, width), x2, mod, w_out[l],
                       ln_g[l][None, :], ln_b[l][None, :], seq=seq, alpha=alpha)
    return x2.reshape(batch, seq, d_model)
```

```python
import functools
import math

import jax
import jax.numpy as jnp
from jax import lax
from jax.experimental import pallas as pl
from jax.experimental.pallas import tpu as pltpu

HEAD_DIM = 64
CHUNK = 64
BLK = 128
LN_EPS = 1e-5

LANES = 128
ROWS_F32 = 8
VMEM_LIMIT_BYTES = 56 * 1024 * 1024

IN_PROJ_ROWS = 1024
IN_PROJ_SUB_ROWS = 512
OUT_PROJ_ROWS = 2048
OUT_PROJ_SUB_ROWS = 256
ADALN_COLS = 512
ATTN_QBLOCKS = 8
ATTN_STATIC_HALF_BLOCKS = 5
ATTN_SKEW = 4
ATTN_LOG_KEEP_FLOOR = -110.0

_F32 = jnp.float32
_BF16 = jnp.bfloat16
_NT = (((1,), (1,)), ((), ()))
_LOG2E = 1.4426950408889634
_MASKED_SCORE = -1e30


def _layer_norm_rows(v, g, b):
    mu = jnp.mean(v, axis=-1, keepdims=True)
    d = v - mu
    var = jnp.mean(d * d, axis=-1, keepdims=True)
    return d * lax.rsqrt(var + LN_EPS) * g + b


def _adaln_kernel(c_ref, w_ref, b_ref, o_ref):
    c_rows = jnp.concatenate([jnp.broadcast_to(c_ref[i:i + 1, :], (ROWS_F32, c_ref.shape[1]))
                              for i in range(c_ref.shape[0])], axis=0)
    o_ref[...] = jnp.dot(c_rows.astype(_BF16), w_ref[...].astype(_BF16),
                         preferred_element_type=_F32) + b_ref[...]


def _adaln_mod(c, w_ada, b_ada):
    batch, d = c.shape
    n = w_ada.shape[1]
    tn = ADALN_COLS
    return pl.pallas_call(
        _adaln_kernel,
        grid=(n // tn,),
        in_specs=[pl.BlockSpec((batch, d), lambda j: (0, 0)),
                  pl.BlockSpec((d, tn), lambda j: (0, j)),
                  pl.BlockSpec((1, tn), lambda j: (0, j))],
        out_specs=pl.BlockSpec((batch * ROWS_F32, tn), lambda j: (0, j)),
        out_shape=jax.ShapeDtypeStruct((batch * ROWS_F32, n), _F32),
        compiler_params=pltpu.CompilerParams(vmem_limit_bytes=VMEM_LIMIT_BYTES),
        name="adaln_mod",
    )(c, w_ada, b_ada)


def _mod_row(mod_ref, step, steps_per_batch):
    first = pl.multiple_of((step // steps_per_batch) * ROWS_F32, ROWS_F32)
    return mod_ref[pl.ds(first, ROWS_F32), :][0:1, :]


def _wprep_kernel(w_ref, wo_ref, wb_ref, wt_ref, wob_ref, *, transposed_groups):
    g = pl.program_id(0)
    wb_ref[...] = w_ref[...].astype(_BF16)

    @pl.when(g == 0)
    def _():
        wob_ref[...] = wo_ref[...].astype(_BF16)

    for t in transposed_groups:
        @pl.when(g == t)
        def _():
            wt_ref[...] = w_ref[...].T.astype(_BF16)


def _prep_weights(w_in, w_out, width, transposed_groups):
    d_model, n = w_in.shape
    q_group, v_group = transposed_groups
    assert q_group < v_group
    return pl.pallas_call(
        functools.partial(_wprep_kernel, transposed_groups=transposed_groups),
        grid=(n // width,),
        in_specs=[pl.BlockSpec((d_model, width), lambda g: (0, g)),
                  pl.BlockSpec(w_out.shape, lambda g: (0, 0))],
        out_specs=[pl.BlockSpec((d_model, width), lambda g: (0, g)),
                   pl.BlockSpec((width, d_model), lambda g: (jnp.where(g > q_group, 1, 0), 0)),
                   pl.BlockSpec(w_out.shape, lambda g: (0, 0))],
        out_shape=[jax.ShapeDtypeStruct((d_model, n), _BF16),
                   jax.ShapeDtypeStruct((2 * width, d_model), _BF16),
                   jax.ShapeDtypeStruct(w_out.shape, _BF16)],
        compiler_params=pltpu.CompilerParams(dimension_semantics=("arbitrary",),
                                             vmem_limit_bytes=VMEM_LIMIT_BYTES),
        name="w_prep",
    )(w_in, w_out)


_G_U, _G_V, _G_ZA, _G_Q, _G_K, _G_VB, _G_ZB = range(7)


def _inproj_kernel(x_ref, mod_ref, w_ref, wt_ref, lng_ref, lnb_ref, ws_ref, bs_ref,
                   oa_ref, qt_ref, k_ref, vt_ref, sb_ref, *, d_model, width, steps_per_batch):
    mod = _mod_row(mod_ref, pl.program_id(0), steps_per_batch)
    shift = mod[:, :d_model]
    scale = 1.0 + mod[:, d_model:2 * d_model]

    row = lax.broadcasted_iota(jnp.int32, (BLK, BLK), 0)
    col = lax.broadcasted_iota(jnp.int32, (BLK, BLK), 1)
    causal = (col // CHUNK) <= (row // CHUNK)
    heads_per_lane_group = LANES // HEAD_DIM
    w_pairs = [jnp.concatenate(
        [jnp.where(causal, ws_ref[heads_per_lane_group * p + i], 0.0).astype(_BF16)
         for i in range(heads_per_lane_group)], axis=1) for p in range(width // LANES)]
    head_of_lane = lax.broadcasted_iota(jnp.int32, (BLK, LANES), 1) // HEAD_DIM
    zero_tile = jnp.zeros((BLK, LANES), _BF16)
    blocks_per_dot = 2
    q_scale = 1.0 / math.sqrt(HEAD_DIM)

    for s0 in range(0, x_ref.shape[0], IN_PROJ_SUB_ROWS):
        sub = slice(s0, s0 + IN_PROJ_SUB_ROWS)
        h = (x_ref[sub, :] * scale + shift).astype(_BF16)

        def group(g):
            return jnp.dot(h, w_ref[:, g * width:(g + 1) * width], preferred_element_type=_F32)

        vn = _layer_norm_rows(jax.nn.gelu(group(_G_V)), lng_ref[...], lnb_ref[...]).astype(_BF16)
        u = jax.nn.gelu(group(_G_U))
        za = group(_G_ZA)
        gate_a = u * (za * jax.nn.sigmoid(za))

        k_ref[sub, :] = group(_G_K).astype(_BF16)
        zb = group(_G_ZB)
        sb_ref[sub, :] = zb * jax.nn.sigmoid(zb)
        qv_t = lax.dot_general(wt_ref[...], h, _NT, preferred_element_type=_F32)
        for r in range(IN_PROJ_SUB_ROWS // BLK):
            cols = slice(r * BLK, (r + 1) * BLK)
            qt_ref[0, s0 // BLK + r] = (qv_t[:width, cols] * q_scale).astype(_BF16)
            vt_ref[0, s0 // BLK + r] = qv_t[width:, cols].astype(_BF16)

        for p in range(width // LANES):
            cols = slice(p * LANES, (p + 1) * LANES)
            for r0 in range(0, IN_PROJ_SUB_ROWS // BLK, blocks_per_dot):
                blocks = [slice(r * BLK, (r + 1) * BLK) for r in range(r0, r0 + blocks_per_dot)]
                stacked = jnp.concatenate(
                    [jnp.concatenate([jnp.where(head_of_lane == i, vn[rows, cols], zero_tile)
                                      for i in range(heads_per_lane_group)], axis=0)
                     for rows in blocks], axis=1)
                res = jnp.dot(w_pairs[p], stacked, preferred_element_type=_F32)
                for i, rows in enumerate(blocks):
                    mixed = res[:, i * LANES:(i + 1) * LANES] + bs_ref[:, cols]
                    oa_ref[s0 + rows.start:s0 + rows.stop, cols] = (gate_a[rows, cols] * mixed).astype(_BF16)


def _in_proj(x2, mod, w_bf16, w_qv_t, ln_v_g, ln_v_b, w_spatial, bias_tile, *, batch, seq):
    m, d_model = x2.shape
    width = ln_v_g.shape[-1]
    tm = IN_PROJ_ROWS
    steps_per_batch = seq // tm
    nblk = seq // BLK
    row_spec = lambda cols: pl.BlockSpec((tm, cols), lambda i: (i, 0))
    full = lambda a: pl.BlockSpec(a.shape, lambda i: (0,) * a.ndim)
    t_spec = pl.BlockSpec((1, tm // BLK, width, BLK),
                          lambda i: (i // steps_per_batch, i % steps_per_batch, 0, 0))
    return pl.pallas_call(
        functools.partial(_inproj_kernel, d_model=d_model, width=width,
                          steps_per_batch=steps_per_batch),
        grid=(m // tm,),
        in_specs=[row_spec(d_model), full(mod), full(w_bf16), full(w_qv_t), full(ln_v_g),
                  full(ln_v_b), full(w_spatial), full(bias_tile)],
        out_specs=[row_spec(width), t_spec, row_spec(width), t_spec, row_spec(width)],
        out_shape=[jax.ShapeDtypeStruct((m, width), _BF16),
                   jax.ShapeDtypeStruct((batch, nblk, width, BLK), _BF16),
                   jax.ShapeDtypeStruct((m, width), _BF16),
                   jax.ShapeDtypeStruct((batch, nblk, width, BLK), _BF16),
                   jax.ShapeDtypeStruct((m, width), _F32)],
        compiler_params=pltpu.CompilerParams(vmem_limit_bytes=VMEM_LIMIT_BYTES),
        name="in_proj",
    )(x2, mod, w_bf16, w_qv_t, ln_v_g, ln_v_b, w_spatial, bias_tile)


def _sb_softplus(z):
    return jnp.maximum(z, 0.0) + jnp.log(1.0 + jnp.exp2(jnp.abs(z) * (-_LOG2E)))


def _bf16_split(v):
    hi = v.astype(_BF16)
    lo = (v - hi.astype(_F32)).astype(_BF16)
    return jnp.concatenate([hi, lo], axis=0)


def _attn_kernel(qt_ref, k_ref, vt_ref, sb_ref, o_ref, acc_ref, dec_ref, *, exit_floor):
    nqb = qt_ref.shape[1]
    width = qt_ref.shape[2]
    n_groups = width // LANES
    pair = 2 * LANES
    half = BLK // 2
    first_qblock = pl.program_id(1) * nqb

    def cum_matrix(nk):
        row = lax.broadcasted_iota(jnp.int32, (nk, nk), 0)
        col = lax.broadcasted_iota(jnp.int32, (nk, nk), 1)
        m = jnp.where(col >= row, 1.0, 0.0).astype(_BF16)
        return jnp.concatenate([m, m], axis=1)

    cum = {BLK: cum_matrix(BLK), half: cum_matrix(half)}
    causal_bias = jnp.where(lax.broadcasted_iota(jnp.int32, (BLK, pair), 0)
                            < (lax.broadcasted_iota(jnp.int32, (BLK, pair), 1) & (LANES - 1)),
                            0.0, _MASKED_SCORE)
    last_row = lax.broadcasted_iota(jnp.int32, (ROWS_F32, pair), 0) == ROWS_F32 - 1
    zero_rows = jnp.zeros((HEAD_DIM, LANES), _BF16)
    zero_half = jnp.zeros((half, pair), _BF16)
    lanes_of = lambda g: slice(g * LANES, (g + 1) * LANES)

    def q_block_diag(qb, g):
        q = qt_ref[0, qb, lanes_of(g), :]
        left = jnp.concatenate([q[:HEAD_DIM], zero_rows], axis=0)
        right = jnp.concatenate([zero_rows, q[HEAD_DIM:]], axis=0)
        return jnp.concatenate([left, right], axis=1)

    def sweep(jobs):
        chains, q_bd, dec, acc = [], {}, {}, {}
        for qb, spans, decay in jobs:
            for g in range(n_groups):
                q_bd[qb, g] = q_block_diag(qb, g)
                dec[qb, g] = None if decay is None else decay[g]
        for d in range(max(len(spans) for _, spans, _ in jobs)):
            for qb, spans, _ in jobs:
                if d < len(spans):
                    chains.extend((qb, g) + spans[d] for g in range(n_groups))
        z, w = {}, {}

        def scores(c):
            qb, g, jc, row0, nk, present, diagonal = chains[c]
            rows = slice(row0, row0 + nk) if isinstance(row0, int) else pl.ds(row0, nk)
            zc = jnp.dot(k_ref[0, jc, rows, lanes_of(g)], q_bd[qb, g], preferred_element_type=_F32)
            z[c] = zc + causal_bias if diagonal else zc

        def weights(c):
            qb, g, jc, row0, nk, present, diagonal = chains[c]
            sp = _sb_softplus(z[c])
            before = dec[qb, g]
            if before is not None:
                before = before if present is None else jnp.where(present, before, -_MASKED_SCORE)
                sp = jnp.concatenate([sp[:nk - ROWS_F32],
                                      sp[nk - ROWS_F32:] + jnp.where(last_row, before, 0.0)], axis=0)
            cs = jnp.dot(cum[nk], _bf16_split(sp), preferred_element_type=_F32)
            wc = jnp.exp(z.pop(c) - cs).astype(_BF16)
            if nk == half:
                if isinstance(row0, int):
                    wc = jnp.concatenate([zero_half, wc] if row0 else [wc, zero_half], axis=0)
                else:
                    wc = jnp.where(row0 > 0, jnp.concatenate([zero_half, wc], axis=0),
                                   jnp.concatenate([wc, zero_half], axis=0))
            w[c] = wc
            total = jnp.broadcast_to(cs[0:1, :], (ROWS_F32, pair))
            if before is not None:
                total = total - before
            if present is not None:
                total = jnp.where(present, total, 0.0)
            dec[qb, g] = total if dec[qb, g] is None else dec[qb, g] + total

        def values(c):
            qb, g, jc = chains[c][:3]
            both = jnp.dot(vt_ref[0, jc, lanes_of(g), :], w.pop(c), preferred_element_type=_F32)
            pv = jnp.concatenate([both[:HEAD_DIM, :LANES], both[HEAD_DIM:, LANES:]], axis=0)
            acc[qb, g] = pv if acc.get((qb, g)) is None else acc[qb, g] + pv

        stages = (scores, weights, values)
        for step in range(len(chains) + ATTN_SKEW * (len(stages) - 1)):
            for s, stage in enumerate(stages):
                c = step - ATTN_SKEW * s
                if 0 <= c < len(chains):
                    stage(c)
        return [([acc[qb, g] for g in range(n_groups)], [dec[qb, g] for g in range(n_groups)])
                for qb, _, _ in jobs]

    pos_inf = jnp.full((ROWS_F32, pair), jnp.inf, _F32)
    q_index = lambda qb: first_qblock + qb

    def static_spans(i_q):
        spans = [(i_q, 0, BLK, None, True)]
        for d in range(1, ATTN_STATIC_HALF_BLOCKS // 2):
            spans.append((jnp.maximum(i_q - d, 0), 0, BLK, i_q - d >= 0, False))
        d = ATTN_STATIC_HALF_BLOCKS // 2
        spans.append((jnp.maximum(i_q - d, 0), half, half, i_q - d >= 0, False))
        return spans

    out = sweep([(qb, static_spans(q_index(qb)), None) for qb in range(nqb)])

    def commit(out, more, accumulate):
        least = pos_inf
        for qb, (pv, decay) in enumerate(out):
            for g in range(n_groups):
                if accumulate:
                    acc_ref[qb, lanes_of(g), :] += pv[g]
                else:
                    acc_ref[qb, lanes_of(g), :] = pv[g]
                dec_ref[qb * n_groups + g] = decay[g]
                least = jnp.minimum(least, jnp.where(more[qb], decay[g], pos_inf))
        return jnp.min(least)

    next_half = lambda qb, n: 2 * q_index(qb) - (ATTN_STATIC_HALF_BLOCKS - 1) - n
    least = commit(out, [next_half(qb, 0) >= 0 for qb in range(nqb)], False)

    def tail_cond(state):
        return -state[1] > exit_floor

    def tail_body(state):
        n = state[0]
        jobs = []
        for qb in range(nqb):
            hb = next_half(qb, n)
            hbc = jnp.maximum(hb, 0)
            row0 = pl.multiple_of((hbc & 1) * half, half)
            jobs.append((qb, [(hbc >> 1, row0, half, hb >= 0, False)],
                         [dec_ref[qb * n_groups + g] for g in range(n_groups)]))
        return n + 1, commit(sweep(jobs), [next_half(qb, n) >= 1 for qb in range(nqb)], True)

    lax.while_loop(tail_cond, tail_body, (jnp.int32(0), least))

    for qb in range(nqb):
        rows = slice(qb * BLK, (qb + 1) * BLK)
        for g in range(n_groups):
            o_ref[0, rows, lanes_of(g)] = (acc_ref[qb, lanes_of(g), :].T
                                           * sb_ref[0, rows, lanes_of(g)]).astype(_BF16)


def _sb_attention(q_t, k_blocks, v_t, sb3, *, exit_floor=ATTN_LOG_KEEP_FLOOR):
    batch, nblk, width, _ = q_t.shape
    nqb = ATTN_QBLOCKS
    tq = nqb * BLK
    resident = lambda a: pl.BlockSpec((1,) + a.shape[1:], lambda b, i: (b, 0, 0, 0))
    return pl.pallas_call(
        functools.partial(_attn_kernel, exit_floor=exit_floor),
        grid=(batch, nblk // nqb),
        in_specs=[pl.BlockSpec((1, nqb, width, BLK), lambda b, i: (b, i, 0, 0)),
                  resident(k_blocks), resident(v_t),
                  pl.BlockSpec((1, tq, width), lambda b, i: (b, i, 0))],
        out_specs=pl.BlockSpec((1, tq, width), lambda b, i: (b, i, 0)),
        out_shape=jax.ShapeDtypeStruct(sb3.shape, _BF16),
        scratch_shapes=[pltpu.VMEM((nqb, width, BLK), _F32),
                        pltpu.VMEM((nqb * width // LANES, ROWS_F32, 2 * LANES), _F32)],
        compiler_params=pltpu.CompilerParams(
            dimension_semantics=("arbitrary", "arbitrary"),
            vmem_limit_bytes=VMEM_LIMIT_BYTES),
        name="sb_attn",
    )(q_t, k_blocks, v_t, sb3)


def _outproj_kernel(oa_ref, ob_ref, x_ref, mod_ref, wb_ref, g_ref, b_ref, o_ref, *,
                    alpha, steps_per_batch):
    width = oa_ref.shape[1]
    d_model = x_ref.shape[1]
    gate = 1.0 + _mod_row(mod_ref, pl.program_id(0), steps_per_batch)[:, 2 * d_model:]
    for r0 in range(0, x_ref.shape[0], OUT_PROJ_SUB_ROWS):
        rows = slice(r0, r0 + OUT_PROJ_SUB_ROWS)
        y = (jnp.dot(oa_ref[rows, :], wb_ref[:width, :], preferred_element_type=_F32)
             + jnp.dot(ob_ref[rows, :], wb_ref[width:, :], preferred_element_type=_F32))
        r = alpha * x_ref[rows, :] + gate * y
        o_ref[rows, :] = _layer_norm_rows(r, g_ref[...], b_ref[...])


def _out_proj(out_a, out_b, x2, mod, w_out, ln_g, ln_b, *, seq, alpha):
    m, d_model = x2.shape
    width = out_a.shape[1]
    tm = OUT_PROJ_ROWS
    steps_per_batch = seq // tm
    row_spec = lambda cols: pl.BlockSpec((tm, cols), lambda i: (i, 0))
    full = lambda a: pl.BlockSpec(a.shape, lambda i: (0,) * a.ndim)
    return pl.pallas_call(
        functools.partial(_outproj_kernel, alpha=alpha, steps_per_batch=steps_per_batch),
        grid=(m // tm,),
        in_specs=[row_spec(width), row_spec(width), row_spec(d_model), full(mod), full(w_out),
                  full(ln_g), full(ln_b)],
        out_specs=row_spec(d_model),
        out_shape=jax.ShapeDtypeStruct((m, d_model), _F32),
        compiler_params=pltpu.CompilerParams(vmem_limit_bytes=VMEM_LIMIT_BYTES),
        name="out_proj",
    )(out_a, out_b, x2, mod, w_out, ln_g, ln_b)


def kernel(x, c, w_ada, b_ada, w_in, ln_v_g, ln_v_b, w_spatial, b_spatial, w_out, ln_g, ln_b):
    batch, seq, d_model = x.shape
    depth = w_ada.shape[0]
    width = ln_v_g.shape[-1]
    n_heads = w_spatial.shape[1]
    assert width == n_heads * HEAD_DIM and w_in.shape[-1] == 7 * width
    assert w_spatial.shape[2:] == (BLK, BLK)
    assert seq % IN_PROJ_ROWS == 0 and seq % OUT_PROJ_ROWS == 0
    assert IN_PROJ_ROWS % IN_PROJ_SUB_ROWS == 0 and IN_PROJ_SUB_ROWS % (2 * BLK) == 0
    assert (seq // BLK) % ATTN_QBLOCKS == 0 and (3 * d_model) % ADALN_COLS == 0
    alpha = (2 * depth) ** 0.25

    x2 = x.reshape(batch * seq, d_model)
    for l in range(depth):
        w_bf16, w_qv_t, w_out_bf16 = _prep_weights(w_in[l], w_out[l], width, (_G_Q, _G_VB))
        bias_tile = jnp.repeat(b_spatial[l].T, HEAD_DIM, axis=1)
        mod = _adaln_mod(c, w_ada[l], b_ada[l][None, :])
        out_a, q_t, k, v_t, sb = _in_proj(
            x2, mod, w_bf16, w_qv_t, ln_v_g[l][None, :], ln_v_b[l][None, :], w_spatial[l],
            bias_tile, batch=batch, seq=seq)
        out_b = _sb_attention(q_t, k.reshape(batch, seq // BLK, BLK, width), v_t,
                              sb.reshape(batch, seq, width))
        x2 = _out_proj(out_a, out_b.reshape(batch * seq, width), x2, mod, w_out_bf16,
                       ln_g[l][None, :], ln_b[l][None, :], seq=seq, alpha=alpha)
    return x2.reshape(batch, seq, d_model)
```

```python
import functools
import math

import jax
import jax.numpy as jnp
from jax import lax
from jax.experimental import pallas as pl
from jax.experimental.pallas import tpu as pltpu

HEAD_DIM = 64
CHUNK = 64
BLK = 128
LN_EPS = 1e-5

LANES = 128
ROWS_F32 = 8
VMEM_LIMIT_BYTES = 56 * 1024 * 1024

IN_PROJ_ROWS = 1024
IN_PROJ_SUB_ROWS = 512
OUT_PROJ_ROWS = 2048
OUT_PROJ_SUB_ROWS = 256
ADALN_COLS = 512
ATTN_QBLOCKS = 8
ATTN_STATIC_HALF_BLOCKS = 5
ATTN_SKEW = 3
ATTN_LOG_KEEP_FLOOR = -110.0

_F32 = jnp.float32
_BF16 = jnp.bfloat16
_NT = (((1,), (1,)), ((), ()))
_LOG2E = 1.4426950408889634
_MASKED_SCORE = -1e30


def _layer_norm_rows(v, g, b):
    mu = jnp.mean(v, axis=-1, keepdims=True)
    d = v - mu
    var = jnp.mean(d * d, axis=-1, keepdims=True)
    return d * lax.rsqrt(var + LN_EPS) * g + b


def _adaln_kernel(c_ref, w_ref, b_ref, o_ref):
    c_rows = jnp.concatenate([jnp.broadcast_to(c_ref[i:i + 1, :], (ROWS_F32, c_ref.shape[1]))
                              for i in range(c_ref.shape[0])], axis=0)
    o_ref[...] = jnp.dot(c_rows.astype(_BF16), w_ref[...].astype(_BF16),
                         preferred_element_type=_F32) + b_ref[...]


def _adaln_mod(c, w_ada, b_ada):
    batch, d = c.shape
    n = w_ada.shape[1]
    tn = ADALN_COLS
    return pl.pallas_call(
        _adaln_kernel,
        grid=(n // tn,),
        in_specs=[pl.BlockSpec((batch, d), lambda j: (0, 0)),
                  pl.BlockSpec((d, tn), lambda j: (0, j)),
                  pl.BlockSpec((1, tn), lambda j: (0, j))],
        out_specs=pl.BlockSpec((batch * ROWS_F32, tn), lambda j: (0, j)),
        out_shape=jax.ShapeDtypeStruct((batch * ROWS_F32, n), _F32),
        compiler_params=pltpu.CompilerParams(vmem_limit_bytes=VMEM_LIMIT_BYTES),
        name="adaln_mod",
    )(c, w_ada, b_ada)


def _mod_row(mod_ref, step, steps_per_batch):
    first = pl.multiple_of((step // steps_per_batch) * ROWS_F32, ROWS_F32)
    return mod_ref[pl.ds(first, ROWS_F32), :][0:1, :]


_G_U, _G_V, _G_ZA, _G_Q, _G_K, _G_VB, _G_ZB = range(7)


def _inproj_kernel(x_ref, mod_ref, w32_ref, lng_ref, lnb_ref, ws_ref, bs_ref,
                   oa_ref, qt_ref, k_ref, vt_ref, sb_ref, w_ref, wt_ref, *,
                   d_model, width, steps_per_batch):
    @pl.when(pl.program_id(0) == 0)
    def _():
        def cast_rows(r, carry):
            rows = pl.ds(pl.multiple_of(r * BLK, BLK), BLK)
            w_ref[rows, :] = w32_ref[rows, :].astype(_BF16)
            return carry
        lax.fori_loop(0, d_model // BLK, cast_rows, 0)
        for i, g in enumerate((_G_Q, _G_VB)):
            for c0 in range(0, width, BLK):
                wt_ref[i * width + c0:i * width + c0 + BLK, :] = (
                    w32_ref[:, g * width + c0:g * width + c0 + BLK].T.astype(_BF16))

    mod = _mod_row(mod_ref, pl.program_id(0), steps_per_batch)
    shift = mod[:, :d_model]
    scale = 1.0 + mod[:, d_model:2 * d_model]

    row = lax.broadcasted_iota(jnp.int32, (BLK, BLK), 0)
    col = lax.broadcasted_iota(jnp.int32, (BLK, BLK), 1)
    causal = (col // CHUNK) <= (row // CHUNK)
    heads_per_lane_group = LANES // HEAD_DIM
    w_pairs = [jnp.concatenate(
        [jnp.where(causal, ws_ref[heads_per_lane_group * p + i], 0.0).astype(_BF16)
         for i in range(heads_per_lane_group)], axis=1) for p in range(width // LANES)]
    head_of_lane = lax.broadcasted_iota(jnp.int32, (BLK, LANES), 1) // HEAD_DIM
    zero_tile = jnp.zeros((BLK, LANES), _BF16)
    blocks_per_dot = 2
    q_scale = 1.0 / math.sqrt(HEAD_DIM)

    for s0 in range(0, x_ref.shape[0], IN_PROJ_SUB_ROWS):
        sub = slice(s0, s0 + IN_PROJ_SUB_ROWS)
        h = (x_ref[sub, :] * scale + shift).astype(_BF16)

        def group(g):
            return jnp.dot(h, w_ref[:, g * width:(g + 1) * width], preferred_element_type=_F32)

        vn = _layer_norm_rows(jax.nn.gelu(group(_G_V)), lng_ref[...], lnb_ref[...]).astype(_BF16)
        u = jax.nn.gelu(group(_G_U))
        za = group(_G_ZA)
        gate_a = u * (za * jax.nn.sigmoid(za))

        k_ref[sub, :] = group(_G_K).astype(_BF16)
        zb = group(_G_ZB)
        sb_ref[sub, :] = zb * jax.nn.sigmoid(zb)
        qv_t = lax.dot_general(wt_ref[...], h, _NT, preferred_element_type=_F32)
        for r in range(IN_PROJ_SUB_ROWS // BLK):
            cols = slice(r * BLK, (r + 1) * BLK)
            qt_ref[0, s0 // BLK + r] = (qv_t[:width, cols] * q_scale).astype(_BF16)
            vt_ref[0, s0 // BLK + r] = qv_t[width:, cols].astype(_BF16)

        for p in range(width // LANES):
            cols = slice(p * LANES, (p + 1) * LANES)
            for r0 in range(0, IN_PROJ_SUB_ROWS // BLK, blocks_per_dot):
                blocks = [slice(r * BLK, (r + 1) * BLK) for r in range(r0, r0 + blocks_per_dot)]
                stacked = jnp.concatenate(
                    [jnp.concatenate([jnp.where(head_of_lane == i, vn[rows, cols], zero_tile)
                                      for i in range(heads_per_lane_group)], axis=0)
                     for rows in blocks], axis=1)
                res = jnp.dot(w_pairs[p], stacked, preferred_element_type=_F32)
                for i, rows in enumerate(blocks):
                    mixed = res[:, i * LANES:(i + 1) * LANES] + bs_ref[:, cols]
                    oa_ref[s0 + rows.start:s0 + rows.stop, cols] = (gate_a[rows, cols] * mixed).astype(_BF16)


def _in_proj(x2, mod, w_in, ln_v_g, ln_v_b, w_spatial, bias_tile, *, batch, seq):
    m, d_model = x2.shape
    width = ln_v_g.shape[-1]
    tm = IN_PROJ_ROWS
    steps_per_batch = seq // tm
    nblk = seq // BLK
    row_spec = lambda cols: pl.BlockSpec((tm, cols), lambda i: (i, 0))
    full = lambda a: pl.BlockSpec(a.shape, lambda i: (0,) * a.ndim)
    t_spec = pl.BlockSpec((1, tm // BLK, width, BLK),
                          lambda i: (i // steps_per_batch, i % steps_per_batch, 0, 0))
    return pl.pallas_call(
        functools.partial(_inproj_kernel, d_model=d_model, width=width,
                          steps_per_batch=steps_per_batch),
        grid=(m // tm,),
        in_specs=[row_spec(d_model), full(mod),
                  pl.BlockSpec(w_in.shape, lambda i: (0, 0), pipeline_mode=pl.Buffered(1)),
                  full(ln_v_g), full(ln_v_b), full(w_spatial), full(bias_tile)],
        out_specs=[row_spec(width), t_spec, row_spec(width), t_spec, row_spec(width)],
        out_shape=[jax.ShapeDtypeStruct((m, width), _BF16),
                   jax.ShapeDtypeStruct((batch, nblk, width, BLK), _BF16),
                   jax.ShapeDtypeStruct((m, width), _BF16),
                   jax.ShapeDtypeStruct((batch, nblk, width, BLK), _BF16),
                   jax.ShapeDtypeStruct((m, width), _F32)],
        scratch_shapes=[pltpu.VMEM(w_in.shape, _BF16),
                        pltpu.VMEM((2 * width, d_model), _BF16)],
        compiler_params=pltpu.CompilerParams(dimension_semantics=("arbitrary",),
                                             vmem_limit_bytes=VMEM_LIMIT_BYTES),
        name="in_proj",
    )(x2, mod, w_in, ln_v_g, ln_v_b, w_spatial, bias_tile)


def _sb_softplus(z):
    return jnp.maximum(z, 0.0) + jnp.log(1.0 + jnp.exp2(jnp.abs(z) * (-_LOG2E)))


def _bf16_split(v):
    hi = v.astype(_BF16)
    lo = (v - hi.astype(_F32)).astype(_BF16)
    return jnp.concatenate([hi, lo], axis=0)


def _attn_kernel(qt_ref, k_ref, vt_ref, sb_ref, o_ref, acc_ref, dec_ref, *, exit_floor):
    nqb = qt_ref.shape[1]
    width = qt_ref.shape[2]
    n_groups = width // LANES
    pair = 2 * LANES
    half = BLK // 2
    first_qblock = pl.program_id(1) * nqb

    def cum_matrix(nk):
        row = lax.broadcasted_iota(jnp.int32, (nk, nk), 0)
        col = lax.broadcasted_iota(jnp.int32, (nk, nk), 1)
        m = jnp.where(col >= row, 1.0, 0.0).astype(_BF16)
        return jnp.concatenate([m, m], axis=1)

    cum = {BLK: cum_matrix(BLK), half: cum_matrix(half)}
    causal_bias = jnp.where(lax.broadcasted_iota(jnp.int32, (BLK, pair), 0)
                            < (lax.broadcasted_iota(jnp.int32, (BLK, pair), 1) & (LANES - 1)),
                            0.0, _MASKED_SCORE)
    last_row = lax.broadcasted_iota(jnp.int32, (ROWS_F32, pair), 0) == ROWS_F32 - 1
    zero_rows = jnp.zeros((HEAD_DIM, LANES), _BF16)
    zero_half = jnp.zeros((half, pair), _BF16)
    lanes_of = lambda g: slice(g * LANES, (g + 1) * LANES)

    def q_block_diag(qb, g):
        q = qt_ref[0, qb, lanes_of(g), :]
        left = jnp.concatenate([q[:HEAD_DIM], zero_rows], axis=0)
        right = jnp.concatenate([zero_rows, q[HEAD_DIM:]], axis=0)
        return jnp.concatenate([left, right], axis=1)

    def sweep(jobs):
        chains, q_bd, dec, acc = [], {}, {}, {}
        for qb, spans, decay in jobs:
            for g in range(n_groups):
                q_bd[qb, g] = q_block_diag(qb, g)
                dec[qb, g] = None if decay is None else decay[g]
        for d in range(max(len(spans) for _, spans, _ in jobs)):
            for qb, spans, _ in jobs:
                if d < len(spans):
                    chains.extend((qb, g) + spans[d] for g in range(n_groups))
        z, w = {}, {}

        def scores(c):
            qb, g, jc, row0, nk, present, diagonal = chains[c]
            rows = slice(row0, row0 + nk) if isinstance(row0, int) else pl.ds(row0, nk)
            zc = jnp.dot(k_ref[0, jc, rows, lanes_of(g)], q_bd[qb, g], preferred_element_type=_F32)
            z[c] = zc + causal_bias if diagonal else zc

        def weights(c):
            qb, g, jc, row0, nk, present, diagonal = chains[c]
            sp = _sb_softplus(z[c])
            before = dec[qb, g]
            if before is not None:
                before = before if present is None else jnp.where(present, before, -_MASKED_SCORE)
                sp = jnp.concatenate([sp[:nk - ROWS_F32],
                                      sp[nk - ROWS_F32:] + jnp.where(last_row, before, 0.0)], axis=0)
            cs = jnp.dot(cum[nk], _bf16_split(sp), preferred_element_type=_F32)
            wc = jnp.exp(z.pop(c) - cs).astype(_BF16)
            if nk == half:
                if isinstance(row0, int):
                    wc = jnp.concatenate([zero_half, wc] if row0 else [wc, zero_half], axis=0)
                else:
                    wc = jnp.where(row0 > 0, jnp.concatenate([zero_half, wc], axis=0),
                                   jnp.concatenate([wc, zero_half], axis=0))
            w[c] = wc
            total = jnp.broadcast_to(cs[0:1, :], (ROWS_F32, pair))
            if before is not None:
                total = total - before
            if present is not None:
                total = jnp.where(present, total, 0.0)
            dec[qb, g] = total if dec[qb, g] is None else dec[qb, g] + total

        def values(c):
            qb, g, jc = chains[c][:3]
            both = jnp.dot(vt_ref[0, jc, lanes_of(g), :], w.pop(c), preferred_element_type=_F32)
            pv = jnp.concatenate([both[:HEAD_DIM, :LANES], both[HEAD_DIM:, LANES:]], axis=0)
            acc[qb, g] = pv if acc.get((qb, g)) is None else acc[qb, g] + pv

        stages = (scores, weights, values)
        for step in range(len(chains) + ATTN_SKEW * (len(stages) - 1)):
            for s, stage in enumerate(stages):
                c = step - ATTN_SKEW * s
                if 0 <= c < len(chains):
                    stage(c)
        return [([acc[qb, g] for g in range(n_groups)], [dec[qb, g] for g in range(n_groups)])
                for qb, _, _ in jobs]

    pos_inf = jnp.full((ROWS_F32, pair), jnp.inf, _F32)
    q_index = lambda qb: first_qblock + qb

    def static_spans(i_q):
        spans = [(i_q, 0, BLK, None, True)]
        for d in range(1, ATTN_STATIC_HALF_BLOCKS // 2):
            spans.append((jnp.maximum(i_q - d, 0), 0, BLK, i_q - d >= 0, False))
        d = ATTN_STATIC_HALF_BLOCKS // 2
        spans.append((jnp.maximum(i_q - d, 0), half, half, i_q - d >= 0, False))
        return spans

    out = sweep([(qb, static_spans(q_index(qb)), None) for qb in range(nqb)])

    def commit(out, more, accumulate):
        least = pos_inf
        for qb, (pv, decay) in enumerate(out):
            for g in range(n_groups):
                if accumulate:
                    acc_ref[qb, lanes_of(g), :] += pv[g]
                else:
                    acc_ref[qb, lanes_of(g), :] = pv[g]
                dec_ref[qb * n_groups + g] = decay[g]
                least = jnp.minimum(least, jnp.where(more[qb], decay[g], pos_inf))
        return jnp.min(least)

    next_half = lambda qb, n: 2 * q_index(qb) - (ATTN_STATIC_HALF_BLOCKS - 1) - n
    least = commit(out, [next_half(qb, 0) >= 0 for qb in range(nqb)], False)

    def tail_cond(state):
        return -state[1] > exit_floor

    def tail_body(state):
        n = state[0]
        jobs = []
        for qb in range(nqb):
            hb = next_half(qb, n)
            hbc = jnp.maximum(hb, 0)
            row0 = pl.multiple_of((hbc & 1) * half, half)
            jobs.append((qb, [(hbc >> 1, row0, half, hb >= 0, False)],
                         [dec_ref[qb * n_groups + g] for g in range(n_groups)]))
        return n + 1, commit(sweep(jobs), [next_half(qb, n) >= 1 for qb in range(nqb)], True)

    lax.while_loop(tail_cond, tail_body, (jnp.int32(0), least))

    for qb in range(nqb):
        rows = slice(qb * BLK, (qb + 1) * BLK)
        for g in range(n_groups):
            o_ref[0, rows, lanes_of(g)] = (acc_ref[qb, lanes_of(g), :].T
                                           * sb_ref[0, rows, lanes_of(g)]).astype(_BF16)


def _sb_attention(q_t, k_blocks, v_t, sb3, *, exit_floor=ATTN_LOG_KEEP_FLOOR):
    batch, nblk, width, _ = q_t.shape
    nqb = ATTN_QBLOCKS
    tq = nqb * BLK
    resident = lambda a: pl.BlockSpec((1,) + a.shape[1:], lambda b, i: (b, 0, 0, 0),
                                      pipeline_mode=pl.Buffered(1))
    return pl.pallas_call(
        functools.partial(_attn_kernel, exit_floor=exit_floor),
        grid=(batch, nblk // nqb),
        in_specs=[pl.BlockSpec((1, nqb, width, BLK), lambda b, i: (b, i, 0, 0)),
                  resident(k_blocks), resident(v_t),
                  pl.BlockSpec((1, tq, width), lambda b, i: (b, i, 0))],
        out_specs=pl.BlockSpec((1, tq, width), lambda b, i: (b, i, 0)),
        out_shape=jax.ShapeDtypeStruct(sb3.shape, _BF16),
        scratch_shapes=[pltpu.VMEM((nqb, width, BLK), _F32),
                        pltpu.VMEM((nqb * width // LANES, ROWS_F32, 2 * LANES), _F32)],
        compiler_params=pltpu.CompilerParams(
            dimension_semantics=("arbitrary", "arbitrary"),
            vmem_limit_bytes=VMEM_LIMIT_BYTES),
        name="sb_attn",
    )(q_t, k_blocks, v_t, sb3)


def _outproj_kernel(oa_ref, ob_ref, x_ref, mod_ref, w_ref, g_ref, b_ref, o_ref, wb_ref, *,
                    alpha, steps_per_batch):
    width = oa_ref.shape[1]
    d_model = x_ref.shape[1]

    @pl.when(pl.program_id(0) == 0)
    def _():
        wb_ref[...] = w_ref[...].astype(_BF16)

    gate = 1.0 + _mod_row(mod_ref, pl.program_id(0), steps_per_batch)[:, 2 * d_model:]
    for r0 in range(0, x_ref.shape[0], OUT_PROJ_SUB_ROWS):
        rows = slice(r0, r0 + OUT_PROJ_SUB_ROWS)
        y = (jnp.dot(oa_ref[rows, :], wb_ref[:width, :], preferred_element_type=_F32)
             + jnp.dot(ob_ref[rows, :], wb_ref[width:, :], preferred_element_type=_F32))
        r = alpha * x_ref[rows, :] + gate * y
        o_ref[rows, :] = _layer_norm_rows(r, g_ref[...], b_ref[...])


def _out_proj(out_a, out_b, x2, mod, w_out, ln_g, ln_b, *, seq, alpha):
    m, d_model = x2.shape
    width = out_a.shape[1]
    tm = OUT_PROJ_ROWS
    steps_per_batch = seq // tm
    row_spec = lambda cols: pl.BlockSpec((tm, cols), lambda i: (i, 0))
    full = lambda a: pl.BlockSpec(a.shape, lambda i: (0,) * a.ndim)
    return pl.pallas_call(
        functools.partial(_outproj_kernel, alpha=alpha, steps_per_batch=steps_per_batch),
        grid=(m // tm,),
        in_specs=[row_spec(width), row_spec(width), row_spec(d_model), full(mod),
                  pl.BlockSpec(w_out.shape, lambda i: (0, 0), pipeline_mode=pl.Buffered(1)),
                  full(ln_g), full(ln_b)],
        out_specs=row_spec(d_model),
        out_shape=jax.ShapeDtypeStruct((m, d_model), _F32),
        scratch_shapes=[pltpu.VMEM(w_out.shape, _BF16)],
        compiler_params=pltpu.CompilerParams(dimension_semantics=("arbitrary",),
                                             vmem_limit_bytes=VMEM_LIMIT_BYTES),
        name="out_proj",
    )(out_a, out_b, x2, mod, w_out, ln_g, ln_b)


def kernel(x, c, w_ada, b_ada, w_in, ln_v_g, ln_v_b, w_spatial, b_spatial, w_out, ln_g, ln_b):
    batch, seq, d_model = x.shape
    depth = w_ada.shape[0]
    width = ln_v_g.shape[-1]
    n_heads = w_spatial.shape[1]
    assert width == n_heads * HEAD_DIM and w_in.shape[-1] == 7 * width
    assert w_spatial.shape[2:] == (BLK, BLK)
    assert seq % IN_PROJ_ROWS == 0 and seq % OUT_PROJ_ROWS == 0
    assert IN_PROJ_ROWS % IN_PROJ_SUB_ROWS == 0 and IN_PROJ_SUB_ROWS % (2 * BLK) == 0
    assert (seq // BLK) % ATTN_QBLOCKS == 0 and (3 * d_model) % ADALN_COLS == 0
    alpha = (2 * depth) ** 0.25

    x2 = x.reshape(batch * seq, d_model)
    for l in range(depth):
        bias_tile = jnp.repeat(b_spatial[l].T, HEAD_DIM, axis=1)
        mod = _adaln_mod(c, w_ada[l], b_ada[l][None, :])
        out_a, q_t, k, v_t, sb = _in_proj(
            x2, mod, w_in[l], ln_v_g[l][None, :], ln_v_b[l][None, :], w_spatial[l], bias_tile,
            batch=batch, seq=seq)
        out_b = _sb_attention(q_t, k.reshape(batch, seq // BLK, BLK, width), v_t,
                              sb.reshape(batch, seq, width))
        x2 = _out_proj(out_a, out_b.reshape(batch * seq, width), x2, mod, w_out[l],
                       ln_g[l][None, :], ln_b[l][None, :], seq=seq, alpha=alpha)
    return x2.reshape(batch, seq, d_model)
```

```python
import functools
import math

import jax
import jax.numpy as jnp
from jax import lax
from jax.experimental import pallas as pl
from jax.experimental.pallas import tpu as pltpu

HEAD_DIM = 64
CHUNK = 64
BLK = 128
LN_EPS = 1e-5

LANES = 128
ROWS_F32 = 8
VMEM_LIMIT_BYTES = 56 * 1024 * 1024

IN_PROJ_ROWS = 1024
IN_PROJ_SUB_ROWS = 512
OUT_PROJ_ROWS = 2048
OUT_PROJ_SUB_ROWS = 256
ADALN_COLS = 1536
ATTN_QBLOCKS = 8
ATTN_STATIC_HALF_BLOCKS = 5
ATTN_SKEW = 3
ATTN_LOG_KEEP_FLOOR = -110.0

_F32 = jnp.float32
_BF16 = jnp.bfloat16
_NT = (((1,), (1,)), ((), ()))
_LOG2E = 1.4426950408889634
_MASKED_SCORE = -1e30


def _layer_norm_rows(v, g, b):
    mu = jnp.mean(v, axis=-1, keepdims=True)
    d = v - mu
    var = jnp.mean(d * d, axis=-1, keepdims=True)
    return d * lax.rsqrt(var + LN_EPS) * g + b


def _adaln_kernel(c_ref, w_ref, b_ref, o_ref):
    c_rows = jnp.concatenate([jnp.broadcast_to(c_ref[i:i + 1, :], (ROWS_F32, c_ref.shape[1]))
                              for i in range(c_ref.shape[0])], axis=0)
    o_ref[...] = jnp.dot(c_rows.astype(_BF16), w_ref[...].astype(_BF16),
                         preferred_element_type=_F32) + b_ref[...]


def _adaln_mod(c, w_ada, b_ada):
    batch, d = c.shape
    n = w_ada.shape[1]
    tn = ADALN_COLS
    return pl.pallas_call(
        _adaln_kernel,
        grid=(n // tn,),
        in_specs=[pl.BlockSpec((batch, d), lambda j: (0, 0)),
                  pl.BlockSpec((d, tn), lambda j: (0, j)),
                  pl.BlockSpec((1, tn), lambda j: (0, j))],
        out_specs=pl.BlockSpec((batch * ROWS_F32, tn), lambda j: (0, j)),
        out_shape=jax.ShapeDtypeStruct((batch * ROWS_F32, n), _F32),
        compiler_params=pltpu.CompilerParams(vmem_limit_bytes=VMEM_LIMIT_BYTES),
        name="adaln_mod",
    )(c, w_ada, b_ada)


def _mod_row(mod_ref, step, steps_per_batch):
    first = pl.multiple_of((step // steps_per_batch) * ROWS_F32, ROWS_F32)
    return mod_ref[pl.ds(first, ROWS_F32), :][0:1, :]


_G_U, _G_V, _G_ZA, _G_Q, _G_K, _G_VB, _G_ZB = range(7)


def _inproj_kernel(x_ref, mod_ref, w32_ref, lng_ref, lnb_ref, ws_ref, bs_ref,
                   oa_ref, qt_ref, k_ref, vt_ref, sb_ref, w_ref, wt_ref, *,
                   d_model, width, steps_per_batch):
    @pl.when(pl.program_id(0) == 0)
    def _():
        def cast_rows(r, carry):
            rows = pl.ds(pl.multiple_of(r * BLK, BLK), BLK)
            w_ref[rows, :] = w32_ref[rows, :].astype(_BF16)
            return carry
        lax.fori_loop(0, d_model // BLK, cast_rows, 0)
        for i, g in enumerate((_G_Q, _G_VB)):
            for c0 in range(0, width, BLK):
                wt_ref[i * width + c0:i * width + c0 + BLK, :] = (
                    w32_ref[:, g * width + c0:g * width + c0 + BLK].T.astype(_BF16))

    mod = _mod_row(mod_ref, pl.program_id(0), steps_per_batch)
    shift = mod[:, :d_model]
    scale = 1.0 + mod[:, d_model:2 * d_model]

    row = lax.broadcasted_iota(jnp.int32, (BLK, BLK), 0)
    col = lax.broadcasted_iota(jnp.int32, (BLK, BLK), 1)
    causal = (col // CHUNK) <= (row // CHUNK)
    heads_per_lane_group = LANES // HEAD_DIM
    w_pairs = [jnp.concatenate(
        [jnp.where(causal, ws_ref[heads_per_lane_group * p + i], 0.0).astype(_BF16)
         for i in range(heads_per_lane_group)], axis=1) for p in range(width // LANES)]
    head_of_lane = lax.broadcasted_iota(jnp.int32, (BLK, LANES), 1) // HEAD_DIM
    zero_tile = jnp.zeros((BLK, LANES), _BF16)
    blocks_per_dot = 2
    q_scale = 1.0 / math.sqrt(HEAD_DIM)

    for s0 in range(0, x_ref.shape[0], IN_PROJ_SUB_ROWS):
        sub = slice(s0, s0 + IN_PROJ_SUB_ROWS)
        h = (x_ref[sub, :] * scale + shift).astype(_BF16)

        def group(g):
            return jnp.dot(h, w_ref[:, g * width:(g + 1) * width], preferred_element_type=_F32)

        vn = _layer_norm_rows(jax.nn.gelu(group(_G_V)), lng_ref[...], lnb_ref[...]).astype(_BF16)
        u = jax.nn.gelu(group(_G_U))
        za = group(_G_ZA)
        gate_a = u * (za * jax.nn.sigmoid(za))

        k_ref[sub, :] = group(_G_K).astype(_BF16)
        zb = group(_G_ZB)
        sb_ref[sub, :] = zb * jax.nn.sigmoid(zb)
        qv_t = lax.dot_general(wt_ref[...], h, _NT, preferred_element_type=_F32)
        for r in range(IN_PROJ_SUB_ROWS // BLK):
            cols = slice(r * BLK, (r + 1) * BLK)
            qt_ref[0, s0 // BLK + r] = (qv_t[:width, cols] * q_scale).astype(_BF16)
            vt_ref[0, s0 // BLK + r] = qv_t[width:, cols].astype(_BF16)

        for p in range(width // LANES):
            cols = slice(p * LANES, (p + 1) * LANES)
            for r0 in range(0, IN_PROJ_SUB_ROWS // BLK, blocks_per_dot):
                blocks = [slice(r * BLK, (r + 1) * BLK) for r in range(r0, r0 + blocks_per_dot)]
                stacked = jnp.concatenate(
                    [jnp.concatenate([jnp.where(head_of_lane == i, vn[rows, cols], zero_tile)
                                      for i in range(heads_per_lane_group)], axis=0)
                     for rows in blocks], axis=1)
                res = jnp.dot(w_pairs[p], stacked, preferred_element_type=_F32)
                for i, rows in enumerate(blocks):
                    mixed = res[:, i * LANES:(i + 1) * LANES] + bs_ref[:, cols]
                    oa_ref[s0 + rows.start:s0 + rows.stop, cols] = (gate_a[rows, cols] * mixed).astype(_BF16)


def _in_proj(x2, mod, w_in, ln_v_g, ln_v_b, w_spatial, bias_tile, *, batch, seq):
    m, d_model = x2.shape
    width = ln_v_g.shape[-1]
    tm = IN_PROJ_ROWS
    steps_per_batch = seq // tm
    nblk = seq // BLK
    row_spec = lambda cols: pl.BlockSpec((tm, cols), lambda i: (i, 0))
    full = lambda a: pl.BlockSpec(a.shape, lambda i: (0,) * a.ndim)
    t_spec = pl.BlockSpec((1, tm // BLK, width, BLK),
                          lambda i: (i // steps_per_batch, i % steps_per_batch, 0, 0))
    return pl.pallas_call(
        functools.partial(_inproj_kernel, d_model=d_model, width=width,
                          steps_per_batch=steps_per_batch),
        grid=(m // tm,),
        in_specs=[row_spec(d_model), full(mod),
                  pl.BlockSpec(w_in.shape, lambda i: (0, 0), pipeline_mode=pl.Buffered(1)),
                  full(ln_v_g), full(ln_v_b), full(w_spatial), full(bias_tile)],
        out_specs=[row_spec(width), t_spec, row_spec(width), t_spec, row_spec(width)],
        out_shape=[jax.ShapeDtypeStruct((m, width), _BF16),
                   jax.ShapeDtypeStruct((batch, nblk, width, BLK), _BF16),
                   jax.ShapeDtypeStruct((m, width), _BF16),
                   jax.ShapeDtypeStruct((batch, nblk, width, BLK), _BF16),
                   jax.ShapeDtypeStruct((m, width), _F32)],
        scratch_shapes=[pltpu.VMEM(w_in.shape, _BF16),
                        pltpu.VMEM((2 * width, d_model), _BF16)],
        compiler_params=pltpu.CompilerParams(dimension_semantics=("arbitrary",),
                                             vmem_limit_bytes=VMEM_LIMIT_BYTES),
        name="in_proj",
    )(x2, mod, w_in, ln_v_g, ln_v_b, w_spatial, bias_tile)


def _sb_softplus(z):
    return jnp.maximum(z, 0.0) + jnp.log(1.0 + jnp.exp2(jnp.abs(z) * (-_LOG2E)))


def _bf16_split(v):
    hi = v.astype(_BF16)
    lo = (v - hi.astype(_F32)).astype(_BF16)
    return jnp.concatenate([hi, lo], axis=0)


def _attn_kernel(qt_ref, k_ref, vt_ref, sb_ref, o_ref, acc_ref, dec_ref, *, exit_floor):
    nqb = qt_ref.shape[1]
    width = qt_ref.shape[2]
    n_groups = width // LANES
    pair = 2 * LANES
    half = BLK // 2
    first_qblock = pl.program_id(1) * nqb

    def cum_matrix(nk):
        row = lax.broadcasted_iota(jnp.int32, (nk, nk), 0)
        col = lax.broadcasted_iota(jnp.int32, (nk, nk), 1)
        m = jnp.where(col >= row, 1.0, 0.0).astype(_BF16)
        return jnp.concatenate([m, m], axis=1)

    cum = {BLK: cum_matrix(BLK), half: cum_matrix(half)}
    causal_bias = jnp.where(lax.broadcasted_iota(jnp.int32, (BLK, pair), 0)
                            < (lax.broadcasted_iota(jnp.int32, (BLK, pair), 1) & (LANES - 1)),
                            0.0, _MASKED_SCORE)
    last_row = lax.broadcasted_iota(jnp.int32, (ROWS_F32, pair), 0) == ROWS_F32 - 1
    zero_rows = jnp.zeros((HEAD_DIM, LANES), _BF16)
    zero_half = jnp.zeros((half, pair), _BF16)
    lanes_of = lambda g: slice(g * LANES, (g + 1) * LANES)

    def q_block_diag(qb, g):
        q = qt_ref[0, qb, lanes_of(g), :]
        left = jnp.concatenate([q[:HEAD_DIM], zero_rows], axis=0)
        right = jnp.concatenate([zero_rows, q[HEAD_DIM:]], axis=0)
        return jnp.concatenate([left, right], axis=1)

    def sweep(jobs):
        chains, q_bd, dec, acc = [], {}, {}, {}
        for qb, spans, decay in jobs:
            for g in range(n_groups):
                q_bd[qb, g] = q_block_diag(qb, g)
                dec[qb, g] = None if decay is None else decay[g]
        for d in range(max(len(spans) for _, spans, _ in jobs)):
            for qb, spans, _ in jobs:
                if d < len(spans):
                    chains.extend((qb, g) + spans[d] for g in range(n_groups))
        z, w = {}, {}

        def scores(c):
            qb, g, jc, row0, nk, present, diagonal = chains[c]
            rows = slice(row0, row0 + nk) if isinstance(row0, int) else pl.ds(row0, nk)
            zc = jnp.dot(k_ref[0, jc, rows, lanes_of(g)], q_bd[qb, g], preferred_element_type=_F32)
            z[c] = zc + causal_bias if diagonal else zc

        def weights(c):
            qb, g, jc, row0, nk, present, diagonal = chains[c]
            sp = _sb_softplus(z[c])
            before = dec[qb, g]
            if before is not None:
                before = before if present is None else jnp.where(present, before, -_MASKED_SCORE)
                sp = jnp.concatenate([sp[:nk - ROWS_F32],
                                      sp[nk - ROWS_F32:] + jnp.where(last_row, before, 0.0)], axis=0)
            cs = jnp.dot(cum[nk], _bf16_split(sp), preferred_element_type=_F32)
            wc = jnp.exp(z.pop(c) - cs).astype(_BF16)
            if nk == half:
                if isinstance(row0, int):
                    wc = jnp.concatenate([zero_half, wc] if row0 else [wc, zero_half], axis=0)
                else:
                    wc = jnp.where(row0 > 0, jnp.concatenate([zero_half, wc], axis=0),
                                   jnp.concatenate([wc, zero_half], axis=0))
            w[c] = wc
            total = jnp.broadcast_to(cs[0:1, :], (ROWS_F32, pair))
            if before is not None:
                total = total - before
            if present is not None:
                total = jnp.where(present, total, 0.0)
            dec[qb, g] = total if dec[qb, g] is None else dec[qb, g] + total

        def values(c):
            qb, g, jc = chains[c][:3]
            both = jnp.dot(vt_ref[0, jc, lanes_of(g), :], w.pop(c), preferred_element_type=_F32)
            pv = jnp.concatenate([both[:HEAD_DIM, :LANES], both[HEAD_DIM:, LANES:]], axis=0)
            acc[qb, g] = pv if acc.get((qb, g)) is None else acc[qb, g] + pv

        stages = (scores, weights, values)
        for step in range(len(chains) + ATTN_SKEW * (len(stages) - 1)):
            for s, stage in enumerate(stages):
                c = step - ATTN_SKEW * s
                if 0 <= c < len(chains):
                    stage(c)
        return [([acc[qb, g] for g in range(n_groups)], [dec[qb, g] for g in range(n_groups)])
                for qb, _, _ in jobs]

    pos_inf = jnp.full((ROWS_F32, pair), jnp.inf, _F32)
    q_index = lambda qb: first_qblock + qb

    def static_spans(i_q):
        spans = [(i_q, 0, BLK, None, True)]
        for d in range(1, ATTN_STATIC_HALF_BLOCKS // 2):
            spans.append((jnp.maximum(i_q - d, 0), 0, BLK, i_q - d >= 0, False))
        d = ATTN_STATIC_HALF_BLOCKS // 2
        spans.append((jnp.maximum(i_q - d, 0), half, half, i_q - d >= 0, False))
        return spans

    out = sweep([(qb, static_spans(q_index(qb)), None) for qb in range(nqb)])

    def commit(out, more, accumulate):
        least = pos_inf
        for qb, (pv, decay) in enumerate(out):
            for g in range(n_groups):
                if accumulate:
                    acc_ref[qb, lanes_of(g), :] += pv[g]
                else:
                    acc_ref[qb, lanes_of(g), :] = pv[g]
                dec_ref[qb * n_groups + g] = decay[g]
                least = jnp.minimum(least, jnp.where(more[qb], decay[g], pos_inf))
        return jnp.min(least)

    next_half = lambda qb, n: 2 * q_index(qb) - (ATTN_STATIC_HALF_BLOCKS - 1) - n
    least = commit(out, [next_half(qb, 0) >= 0 for qb in range(nqb)], False)

    def tail_cond(state):
        return -state[1] > exit_floor

    def tail_body(state):
        n = state[0]
        jobs = []
        for qb in range(nqb):
            hb = next_half(qb, n)
            hbc = jnp.maximum(hb, 0)
            row0 = pl.multiple_of((hbc & 1) * half, half)
            jobs.append((qb, [(hbc >> 1, row0, half, hb >= 0, False)],
                         [dec_ref[qb * n_groups + g] for g in range(n_groups)]))
        return n + 1, commit(sweep(jobs), [next_half(qb, n) >= 1 for qb in range(nqb)], True)

    lax.while_loop(tail_cond, tail_body, (jnp.int32(0), least))

    for qb in range(nqb):
        rows = slice(qb * BLK, (qb + 1) * BLK)
        for g in range(n_groups):
            o_ref[0, rows, lanes_of(g)] = (acc_ref[qb, lanes_of(g), :].T
                                           * sb_ref[0, rows, lanes_of(g)]).astype(_BF16)


def _sb_attention(q_t, k_blocks, v_t, sb3, *, exit_floor=ATTN_LOG_KEEP_FLOOR):
    batch, nblk, width, _ = q_t.shape
    nqb = ATTN_QBLOCKS
    tq = nqb * BLK
    resident = lambda a: pl.BlockSpec((1,) + a.shape[1:], lambda b, i: (b, 0, 0, 0))
    return pl.pallas_call(
        functools.partial(_attn_kernel, exit_floor=exit_floor),
        grid=(batch, nblk // nqb),
        in_specs=[pl.BlockSpec((1, nqb, width, BLK), lambda b, i: (b, i, 0, 0)),
                  resident(k_blocks), resident(v_t),
                  pl.BlockSpec((1, tq, width), lambda b, i: (b, i, 0))],
        out_specs=pl.BlockSpec((1, tq, width), lambda b, i: (b, i, 0)),
        out_shape=jax.ShapeDtypeStruct(sb3.shape, _BF16),
        scratch_shapes=[pltpu.VMEM((nqb, width, BLK), _F32),
                        pltpu.VMEM((nqb * width // LANES, ROWS_F32, 2 * LANES), _F32)],
        compiler_params=pltpu.CompilerParams(
            dimension_semantics=("arbitrary", "arbitrary"),
            vmem_limit_bytes=VMEM_LIMIT_BYTES),
        name="sb_attn",
    )(q_t, k_blocks, v_t, sb3)


def _outproj_kernel(oa_ref, ob_ref, x_ref, mod_ref, w_ref, g_ref, b_ref, o_ref, wb_ref, *,
                    alpha, steps_per_batch):
    width = oa_ref.shape[1]
    d_model = x_ref.shape[1]

    @pl.when(pl.program_id(0) == 0)
    def _():
        wb_ref[...] = w_ref[...].astype(_BF16)

    gate = 1.0 + _mod_row(mod_ref, pl.program_id(0), steps_per_batch)[:, 2 * d_model:]
    for r0 in range(0, x_ref.shape[0], OUT_PROJ_SUB_ROWS):
        rows = slice(r0, r0 + OUT_PROJ_SUB_ROWS)
        y = (jnp.dot(oa_ref[rows, :], wb_ref[:width, :], preferred_element_type=_F32)
             + jnp.dot(ob_ref[rows, :], wb_ref[width:, :], preferred_element_type=_F32))
        r = alpha * x_ref[rows, :] + gate * y
        o_ref[rows, :] = _layer_norm_rows(r, g_ref[...], b_ref[...])


def _out_proj(out_a, out_b, x2, mod, w_out, ln_g, ln_b, *, seq, alpha):
    m, d_model = x2.shape
    width = out_a.shape[1]
    tm = OUT_PROJ_ROWS
    steps_per_batch = seq // tm
    row_spec = lambda cols: pl.BlockSpec((tm, cols), lambda i: (i, 0))
    full = lambda a: pl.BlockSpec(a.shape, lambda i: (0,) * a.ndim)
    return pl.pallas_call(
        functools.partial(_outproj_kernel, alpha=alpha, steps_per_batch=steps_per_batch),
        grid=(m // tm,),
        in_specs=[row_spec(width), row_spec(width), row_spec(d_model), full(mod),
                  pl.BlockSpec(w_out.shape, lambda i: (0, 0), pipeline_mode=pl.Buffered(1)),
                  full(ln_g), full(ln_b)],
        out_specs=row_spec(d_model),
        out_shape=jax.ShapeDtypeStruct((m, d_model), _F32),
        scratch_shapes=[pltpu.VMEM(w_out.shape, _BF16)],
        compiler_params=pltpu.CompilerParams(dimension_semantics=("arbitrary",),
                                             vmem_limit_bytes=VMEM_LIMIT_BYTES),
        name="out_proj",
    )(out_a, out_b, x2, mod, w_out, ln_g, ln_b)


def kernel(x, c, w_ada, b_ada, w_in, ln_v_g, ln_v_b, w_spatial, b_spatial, w_out, ln_g, ln_b):
    batch, seq, d_model = x.shape
    depth = w_ada.shape[0]
    width = ln_v_g.shape[-1]
    n_heads = w_spatial.shape[1]
    assert width == n_heads * HEAD_DIM and w_in.shape[-1] == 7 * width
    assert w_spatial.shape[2:] == (BLK, BLK)
    assert seq % IN_PROJ_ROWS == 0 and seq % OUT_PROJ_ROWS == 0
    assert IN_PROJ_ROWS % IN_PROJ_SUB_ROWS == 0 and IN_PROJ_SUB_ROWS % (2 * BLK) == 0
    assert (seq // BLK) % ATTN_QBLOCKS == 0 and (3 * d_model) % ADALN_COLS == 0
    alpha = (2 * depth) ** 0.25

    x2 = x.reshape(batch * seq, d_model)
    for l in range(depth):
        bias_tile = jnp.repeat(b_spatial[l].T, HEAD_DIM, axis=1)
        mod = _adaln_mod(c, w_ada[l], b_ada[l][None, :])
        out_a, q_t, k, v_t, sb = _in_proj(
            x2, mod, w_in[l], ln_v_g[l][None, :], ln_v_b[l][None, :], w_spatial[l], bias_tile,
            batch=batch, seq=seq)
        out_b = _sb_attention(q_t, k.reshape(batch, seq // BLK, BLK, width), v_t,
                              sb.reshape(batch, seq, width))
        x2 = _out_proj(out_a, out_b.reshape(batch * seq, width), x2, mod, w_out[l],
                       ln_g[l][None, :], ln_b[l][None, :], seq=seq, alpha=alpha)
    return x2.reshape(batch, seq, d_model)
```

```python
import functools
import math

import jax
import jax.numpy as jnp
from jax import lax
from jax.experimental import pallas as pl
from jax.experimental.pallas import tpu as pltpu

HEAD_DIM = 64
CHUNK = 64
BLK = 128
LN_EPS = 1e-5

LANES = 128
ROWS_F32 = 8
VMEM_LIMIT_BYTES = 56 * 1024 * 1024

IN_PROJ_ROWS = 1024
IN_PROJ_SUB_ROWS = 512
OUT_PROJ_ROWS = 2048
OUT_PROJ_SUB_ROWS = 256
ADALN_COLS = 1536
ATTN_QBLOCKS = 8
ATTN_STATIC_HALF_BLOCKS = 5
ATTN_SKEW = 3
ATTN_LOG_KEEP_FLOOR = -110.0

_F32 = jnp.float32
_BF16 = jnp.bfloat16
_NT = (((1,), (1,)), ((), ()))
_LOG2E = 1.4426950408889634
_MASKED_SCORE = -1e30


def _layer_norm_rows(v, g, b):
    mu = jnp.mean(v, axis=-1, keepdims=True)
    d = v - mu
    var = jnp.mean(d * d, axis=-1, keepdims=True)
    return d * lax.rsqrt(var + LN_EPS) * g + b


def _adaln_kernel(c_ref, w_ref, b_ref, o_ref):
    c_rows = jnp.concatenate([jnp.broadcast_to(c_ref[i:i + 1, :], (ROWS_F32, c_ref.shape[1]))
                              for i in range(c_ref.shape[0])], axis=0)
    o_ref[...] = jnp.dot(c_rows.astype(_BF16), w_ref[...].astype(_BF16),
                         preferred_element_type=_F32) + b_ref[...]


def _adaln_mod(c, w_ada, b_ada):
    batch, d = c.shape
    n = w_ada.shape[1]
    tn = ADALN_COLS
    return pl.pallas_call(
        _adaln_kernel,
        grid=(n // tn,),
        in_specs=[pl.BlockSpec((batch, d), lambda j: (0, 0)),
                  pl.BlockSpec((d, tn), lambda j: (0, j)),
                  pl.BlockSpec((1, tn), lambda j: (0, j))],
        out_specs=pl.BlockSpec((batch * ROWS_F32, tn), lambda j: (0, j)),
        out_shape=jax.ShapeDtypeStruct((batch * ROWS_F32, n), _F32),
        compiler_params=pltpu.CompilerParams(vmem_limit_bytes=VMEM_LIMIT_BYTES),
        name="adaln_mod",
    )(c, w_ada, b_ada)


def _mod_row(mod_ref, step, steps_per_batch):
    first = pl.multiple_of((step // steps_per_batch) * ROWS_F32, ROWS_F32)
    return mod_ref[pl.ds(first, ROWS_F32), :][0:1, :]


_G_U, _G_V, _G_ZA, _G_Q, _G_K, _G_VB, _G_ZB = range(7)


def _inproj_kernel(x_ref, mod_ref, w32_ref, lng_ref, lnb_ref, ws_ref, bs_ref,
                   oa_ref, qt_ref, k_ref, vt_ref, sb_ref, w_ref, wt_ref, *,
                   d_model, width, steps_per_batch):
    @pl.when(pl.program_id(0) == 0)
    def _():
        def cast_rows(r, carry):
            rows = pl.ds(pl.multiple_of(r * BLK, BLK), BLK)
            w_ref[rows, :] = w32_ref[rows, :].astype(_BF16)
            return carry
        lax.fori_loop(0, d_model // BLK, cast_rows, 0)
        for i, g in enumerate((_G_Q, _G_VB)):
            for c0 in range(0, width, BLK):
                wt_ref[i * width + c0:i * width + c0 + BLK, :] = (
                    w32_ref[:, g * width + c0:g * width + c0 + BLK].T.astype(_BF16))

    mod = _mod_row(mod_ref, pl.program_id(0), steps_per_batch)
    shift = mod[:, :d_model]
    scale = 1.0 + mod[:, d_model:2 * d_model]

    row = lax.broadcasted_iota(jnp.int32, (BLK, BLK), 0)
    col = lax.broadcasted_iota(jnp.int32, (BLK, BLK), 1)
    causal = (col // CHUNK) <= (row // CHUNK)
    heads_per_lane_group = LANES // HEAD_DIM
    w_pairs = [jnp.concatenate(
        [jnp.where(causal, ws_ref[heads_per_lane_group * p + i], 0.0).astype(_BF16)
         for i in range(heads_per_lane_group)], axis=1) for p in range(width // LANES)]
    head_of_lane = lax.broadcasted_iota(jnp.int32, (BLK, LANES), 1) // HEAD_DIM
    zero_tile = jnp.zeros((BLK, LANES), _BF16)
    blocks_per_dot = 2
    q_scale = 1.0 / math.sqrt(HEAD_DIM)

    for s0 in range(0, x_ref.shape[0], IN_PROJ_SUB_ROWS):
        sub = slice(s0, s0 + IN_PROJ_SUB_ROWS)
        h = (x_ref[sub, :] * scale + shift).astype(_BF16)

        def group(g):
            return jnp.dot(h, w_ref[:, g * width:(g + 1) * width], preferred_element_type=_F32)

        vn = _layer_norm_rows(jax.nn.gelu(group(_G_V)), lng_ref[...], lnb_ref[...]).astype(_BF16)
        u = jax.nn.gelu(group(_G_U))
        za = group(_G_ZA)
        gate_a = u * (za * jax.nn.sigmoid(za))

        k_ref[sub, :] = group(_G_K).astype(_BF16)
        zb = group(_G_ZB)
        sb_ref[sub, :] = zb * jax.nn.sigmoid(zb)
        qv_t = lax.dot_general(wt_ref[...], h, _NT, preferred_element_type=_F32)
        for r in range(IN_PROJ_SUB_ROWS // BLK):
            cols = slice(r * BLK, (r + 1) * BLK)
            qt_ref[0, s0 // BLK + r] = (qv_t[:width, cols] * q_scale).astype(_BF16)
            vt_ref[0, s0 // BLK + r] = qv_t[width:, cols].astype(_BF16)

        for p in range(width // LANES):
            cols = slice(p * LANES, (p + 1) * LANES)
            for r0 in range(0, IN_PROJ_SUB_ROWS // BLK, blocks_per_dot):
                blocks = [slice(r * BLK, (r + 1) * BLK) for r in range(r0, r0 + blocks_per_dot)]
                stacked = jnp.concatenate(
                    [jnp.concatenate([jnp.where(head_of_lane == i, vn[rows, cols], zero_tile)
                                      for i in range(heads_per_lane_group)], axis=0)
                     for rows in blocks], axis=1)
                res = jnp.dot(w_pairs[p], stacked, preferred_element_type=_F32)
                for i, rows in enumerate(blocks):
                    mixed = res[:, i * LANES:(i + 1) * LANES] + bs_ref[:, cols]
                    oa_ref[s0 + rows.start:s0 + rows.stop, cols] = (gate_a[rows, cols] * mixed).astype(_BF16)


def _in_proj(x2, mod, w_in, ln_v_g, ln_v_b, w_spatial, bias_tile, *, batch, seq):
    m, d_model = x2.shape
    width = ln_v_g.shape[-1]
    tm = IN_PROJ_ROWS
    steps_per_batch = seq // tm
    nblk = seq // BLK
    row_spec = lambda cols: pl.BlockSpec((tm, cols), lambda i: (i, 0))
    full = lambda a: pl.BlockSpec(a.shape, lambda i: (0,) * a.ndim)
    t_spec = pl.BlockSpec((1, tm // BLK, width, BLK),
                          lambda i: (i // steps_per_batch, i % steps_per_batch, 0, 0))
    return pl.pallas_call(
        functools.partial(_inproj_kernel, d_model=d_model, width=width,
                          steps_per_batch=steps_per_batch),
        grid=(m // tm,),
        in_specs=[row_spec(d_model), full(mod),
                  pl.BlockSpec(w_in.shape, lambda i: (0, 0), pipeline_mode=pl.Buffered(1)),
                  full(ln_v_g), full(ln_v_b), full(w_spatial), full(bias_tile)],
        out_specs=[row_spec(width), t_spec, row_spec(width), t_spec, row_spec(width)],
        out_shape=[jax.ShapeDtypeStruct((m, width), _BF16),
                   jax.ShapeDtypeStruct((batch, nblk, width, BLK), _BF16),
                   jax.ShapeDtypeStruct((m, width), _BF16),
                   jax.ShapeDtypeStruct((batch, nblk, width, BLK), _BF16),
                   jax.ShapeDtypeStruct((m, width), _F32)],
        scratch_shapes=[pltpu.VMEM(w_in.shape, _BF16),
                        pltpu.VMEM((2 * width, d_model), _BF16)],
        compiler_params=pltpu.CompilerParams(dimension_semantics=("arbitrary",),
                                             vmem_limit_bytes=VMEM_LIMIT_BYTES),
        name="in_proj",
    )(x2, mod, w_in, ln_v_g, ln_v_b, w_spatial, bias_tile)


def _sb_softplus(z):
    return jnp.maximum(z, 0.0) + jnp.log(1.0 + jnp.exp2(jnp.abs(z) * (-_LOG2E)))


def _bf16_split(v):
    hi = v.astype(_BF16)
    lo = (v - hi.astype(_F32)).astype(_BF16)
    return jnp.concatenate([hi, lo], axis=0)


def _attn_kernel(qt_ref, k_ref, vt_ref, sb_ref, o_ref, acc_ref, dec_ref, *, exit_floor):
    nqb = qt_ref.shape[1]
    width = qt_ref.shape[2]
    n_groups = width // LANES
    pair = 2 * LANES
    half = BLK // 2
    first_qblock = pl.program_id(1) * nqb

    def cum_matrix(nk):
        row = lax.broadcasted_iota(jnp.int32, (nk, nk), 0)
        col = lax.broadcasted_iota(jnp.int32, (nk, nk), 1)
        m = jnp.where(col >= row, 1.0, 0.0).astype(_BF16)
        return jnp.concatenate([m, m], axis=1)

    cum = {BLK: cum_matrix(BLK), half: cum_matrix(half)}
    causal_bias = jnp.where(lax.broadcasted_iota(jnp.int32, (BLK, pair), 0)
                            < (lax.broadcasted_iota(jnp.int32, (BLK, pair), 1) & (LANES - 1)),
                            0.0, _MASKED_SCORE)
    last_row = lax.broadcasted_iota(jnp.int32, (ROWS_F32, pair), 0) == ROWS_F32 - 1
    zero_rows = jnp.zeros((HEAD_DIM, LANES), _BF16)
    zero_half = jnp.zeros((half, pair), _BF16)
    lanes_of = lambda g: slice(g * LANES, (g + 1) * LANES)

    def q_block_diag(qb, g):
        q = qt_ref[0, qb, lanes_of(g), :]
        left = jnp.concatenate([q[:HEAD_DIM], zero_rows], axis=0)
        right = jnp.concatenate([zero_rows, q[HEAD_DIM:]], axis=0)
        return jnp.concatenate([left, right], axis=1)

    def sweep(jobs):
        chains, q_bd, dec, acc = [], {}, {}, {}
        for qb, spans, decay in jobs:
            for g in range(n_groups):
                q_bd[qb, g] = q_block_diag(qb, g)
                dec[qb, g] = None if decay is None else decay[g]
        for d in range(max(len(spans) for _, spans, _ in jobs)):
            for qb, spans, _ in jobs:
                if d < len(spans):
                    chains.extend((qb, g) + spans[d] for g in range(n_groups))
        z, w = {}, {}

        def scores(c):
            qb, g, jc, row0, nk, present, diagonal = chains[c]
            rows = slice(row0, row0 + nk) if isinstance(row0, int) else pl.ds(row0, nk)
            zc = jnp.dot(k_ref[0, jc, rows, lanes_of(g)], q_bd[qb, g], preferred_element_type=_F32)
            z[c] = zc + causal_bias if diagonal else zc

        def weights(c):
            qb, g, jc, row0, nk, present, diagonal = chains[c]
            sp = _sb_softplus(z[c])
            before = dec[qb, g]
            if before is not None:
                before = before if present is None else jnp.where(present, before, -_MASKED_SCORE)
                sp = jnp.concatenate([sp[:nk - ROWS_F32],
                                      sp[nk - ROWS_F32:] + jnp.where(last_row, before, 0.0)], axis=0)
            cs = jnp.dot(cum[nk], _bf16_split(sp), preferred_element_type=_F32)
            wc = jnp.exp(z.pop(c) - cs).astype(_BF16)
            if nk == half:
                if isinstance(row0, int):
                    wc = jnp.concatenate([zero_half, wc] if row0 else [wc, zero_half], axis=0)
                else:
                    wc = jnp.where(row0 > 0, jnp.concatenate([zero_half, wc], axis=0),
                                   jnp.concatenate([wc, zero_half], axis=0))
            w[c] = wc
            total = jnp.broadcast_to(cs[0:1, :], (ROWS_F32, pair))
            if before is not None:
                total = total - before
            if present is not None:
                total = jnp.where(present, total, 0.0)
            dec[qb, g] = total if dec[qb, g] is None else dec[qb, g] + total

        def values(c):
            qb, g, jc = chains[c][:3]
            both = jnp.dot(vt_ref[0, jc, lanes_of(g), :], w.pop(c), preferred_element_type=_F32)
            pv = jnp.concatenate([both[:HEAD_DIM, :LANES], both[HEAD_DIM:, LANES:]], axis=0)
            acc[qb, g] = pv if acc.get((qb, g)) is None else acc[qb, g] + pv

        stages = (scores, weights, values)
        for step in range(len(chains) + ATTN_SKEW * (len(stages) - 1)):
            for s, stage in enumerate(stages):
                c = step - ATTN_SKEW * s
                if 0 <= c < len(chains):
                    stage(c)
        return [([acc[qb, g] for g in range(n_groups)], [dec[qb, g] for g in range(n_groups)])
                for qb, _, _ in jobs]

    pos_inf = jnp.full((ROWS_F32, pair), jnp.inf, _F32)
    q_index = lambda qb: first_qblock + qb

    def static_spans(i_q):
        spans = [(i_q, 0, BLK, None, True)]
        for d in range(1, ATTN_STATIC_HALF_BLOCKS // 2):
            spans.append((jnp.maximum(i_q - d, 0), 0, BLK, i_q - d >= 0, False))
        d = ATTN_STATIC_HALF_BLOCKS // 2
        spans.append((jnp.maximum(i_q - d, 0), half, half, i_q - d >= 0, False))
        return spans

    out = sweep([(qb, static_spans(q_index(qb)), None) for qb in range(nqb)])

    def commit(out, more, accumulate):
        least = pos_inf
        for qb, (pv, decay) in enumerate(out):
            for g in range(n_groups):
                if accumulate:
                    acc_ref[qb, lanes_of(g), :] += pv[g]
                else:
                    acc_ref[qb, lanes_of(g), :] = pv[g]
                dec_ref[qb * n_groups + g] = decay[g]
                least = jnp.minimum(least, jnp.where(more[qb], decay[g], pos_inf))
        return jnp.min(least)

    def write_output():
        for qb in range(nqb):
            rows = slice(qb * BLK, (qb + 1) * BLK)
            for g in range(n_groups):
                o_ref[0, rows, lanes_of(g)] = (acc_ref[qb, lanes_of(g), :].T
                                               * sb_ref[0, rows, lanes_of(g)]).astype(_BF16)

    next_half = lambda qb, n: 2 * q_index(qb) - (ATTN_STATIC_HALF_BLOCKS - 1) - n
    least = commit(out, [next_half(qb, 0) >= 0 for qb in range(nqb)], False)
    write_output()

    def tail_cond(state):
        return -state[1] > exit_floor

    def tail_body(state):
        n = state[0]
        jobs = []
        for qb in range(nqb):
            hb = next_half(qb, n)
            hbc = jnp.maximum(hb, 0)
            row0 = pl.multiple_of((hbc & 1) * half, half)
            jobs.append((qb, [(hbc >> 1, row0, half, hb >= 0, False)],
                         [dec_ref[qb * n_groups + g] for g in range(n_groups)]))
        return n + 1, commit(sweep(jobs), [next_half(qb, n) >= 1 for qb in range(nqb)], True)

    tail_iterations, _ = lax.while_loop(tail_cond, tail_body, (jnp.int32(0), least))

    @pl.when(tail_iterations > 0)
    def _():
        write_output()


def _sb_attention(q_t, k_blocks, v_t, sb3, *, exit_floor=ATTN_LOG_KEEP_FLOOR):
    batch, nblk, width, _ = q_t.shape
    nqb = ATTN_QBLOCKS
    tq = nqb * BLK
    resident = lambda a: pl.BlockSpec((1,) + a.shape[1:], lambda b, i: (b, 0, 0, 0))
    return pl.pallas_call(
        functools.partial(_attn_kernel, exit_floor=exit_floor),
        grid=(batch, nblk // nqb),
        in_specs=[pl.BlockSpec((1, nqb, width, BLK), lambda b, i: (b, i, 0, 0)),
                  resident(k_blocks), resident(v_t),
                  pl.BlockSpec((1, tq, width), lambda b, i: (b, i, 0))],
        out_specs=pl.BlockSpec((1, tq, width), lambda b, i: (b, i, 0)),
        out_shape=jax.ShapeDtypeStruct(sb3.shape, _BF16),
        scratch_shapes=[pltpu.VMEM((nqb, width, BLK), _F32),
                        pltpu.VMEM((nqb * width // LANES, ROWS_F32, 2 * LANES), _F32)],
        compiler_params=pltpu.CompilerParams(
            dimension_semantics=("arbitrary", "arbitrary"),
            vmem_limit_bytes=VMEM_LIMIT_BYTES),
        name="sb_attn",
    )(q_t, k_blocks, v_t, sb3)


def _outproj_kernel(oa_ref, ob_ref, x_ref, mod_ref, w_ref, g_ref, b_ref, o_ref, wb_ref, *,
                    alpha, steps_per_batch):
    width = oa_ref.shape[1]
    d_model = x_ref.shape[1]

    @pl.when(pl.program_id(0) == 0)
    def _():
        wb_ref[...] = w_ref[...].astype(_BF16)

    gate = 1.0 + _mod_row(mod_ref, pl.program_id(0), steps_per_batch)[:, 2 * d_model:]
    for r0 in range(0, x_ref.shape[0], OUT_PROJ_SUB_ROWS):
        rows = slice(r0, r0 + OUT_PROJ_SUB_ROWS)
        y = (jnp.dot(oa_ref[rows, :], wb_ref[:width, :], preferred_element_type=_F32)
             + jnp.dot(ob_ref[rows, :], wb_ref[width:, :], preferred_element_type=_F32))
        r = alpha * x_ref[rows, :] + gate * y
        o_ref[rows, :] = _layer_norm_rows(r, g_ref[...], b_ref[...])


def _out_proj(out_a, out_b, x2, mod, w_out, ln_g, ln_b, *, seq, alpha):
    m, d_model = x2.shape
    width = out_a.shape[1]
    tm = OUT_PROJ_ROWS
    steps_per_batch = seq // tm
    row_spec = lambda cols: pl.BlockSpec((tm, cols), lambda i: (i, 0))
    full = lambda a: pl.BlockSpec(a.shape, lambda i: (0,) * a.ndim)
    return pl.pallas_call(
        functools.partial(_outproj_kernel, alpha=alpha, steps_per_batch=steps_per_batch),
        grid=(m // tm,),
        in_specs=[row_spec(width), row_spec(width), row_spec(d_model), full(mod),
                  pl.BlockSpec(w_out.shape, lambda i: (0, 0), pipeline_mode=pl.Buffered(1)),
                  full(ln_g), full(ln_b)],
        out_specs=row_spec(d_model),
        out_shape=jax.ShapeDtypeStruct((m, d_model), _F32),
        scratch_shapes=[pltpu.VMEM(w_out.shape, _BF16)],
        compiler_params=pltpu.CompilerParams(dimension_semantics=("arbitrary",),
                                             vmem_limit_bytes=VMEM_LIMIT_BYTES),
        name="out_proj",
    )(out_a, out_b, x2, mod, w_out, ln_g, ln_b)


def kernel(x, c, w_ada, b_ada, w_in, ln_v_g, ln_v_b, w_spatial, b_spatial, w_out, ln_g, ln_b):
    batch, seq, d_model = x.shape
    depth = w_ada.shape[0]
    width = ln_v_g.shape[-1]
    n_heads = w_spatial.shape[1]
    assert width == n_heads * HEAD_DIM and w_in.shape[-1] == 7 * width
    assert w_spatial.shape[2:] == (BLK, BLK)
    assert seq % IN_PROJ_ROWS == 0 and seq % OUT_PROJ_ROWS == 0
    assert IN_PROJ_ROWS % IN_PROJ_SUB_ROWS == 0 and IN_PROJ_SUB_ROWS % (2 * BLK) == 0
    assert (seq // BLK) % ATTN_QBLOCKS == 0 and (3 * d_model) % ADALN_COLS == 0
    alpha = (2 * depth) ** 0.25

    x2 = x.reshape(batch * seq, d_model)
    for l in range(depth):
        bias_tile = jnp.repeat(b_spatial[l].T, HEAD_DIM, axis=1)
        mod = _adaln_mod(c, w_ada[l], b_ada[l][None, :])
        out_a, q_t, k, v_t, sb = _in_proj(
            x2, mod, w_in[l], ln_v_g[l][None, :], ln_v_b[l][None, :], w_spatial[l], bias_tile,
            batch=batch, seq=seq)
        out_b = _sb_attention(q_t, k.reshape(batch, seq // BLK, BLK, width), v_t,
                              sb.reshape(batch, seq, width))
        x2 = _out_proj(out_a, out_b.reshape(batch * seq, width), x2, mod, w_out[l],
                       ln_g[l][None, :], ln_b[l][None, :], seq=seq, alpha=alpha)
    return x2.reshape(batch, seq, d_model)
```

```python
import functools
import math

import jax
import jax.numpy as jnp
from jax import lax
from jax.experimental import pallas as pl
from jax.experimental.pallas import tpu as pltpu

HEAD_DIM = 64
CHUNK = 64
BLK = 128
LN_EPS = 1e-5

LANES = 128
ROWS_F32 = 8
VMEM_LIMIT_BYTES = 56 * 1024 * 1024

IN_PROJ_ROWS = 1024
IN_PROJ_SUB_ROWS = 512
OUT_PROJ_ROWS = 2048
OUT_PROJ_SUB_ROWS = 256
ADALN_COLS = 1536
ATTN_QBLOCKS = 8
ATTN_STATIC_HALF_BLOCKS = 5
ATTN_SKEW = 3
ATTN_LOG_KEEP_FLOOR = -float("inf")

_F32 = jnp.float32
_BF16 = jnp.bfloat16
_NT = (((1,), (1,)), ((), ()))
_LOG2E = 1.4426950408889634
_MASKED_SCORE = -1e30


def _layer_norm_rows(v, g, b):
    mu = jnp.mean(v, axis=-1, keepdims=True)
    d = v - mu
    var = jnp.mean(d * d, axis=-1, keepdims=True)
    return d * lax.rsqrt(var + LN_EPS) * g + b


def _adaln_kernel(c_ref, w_ref, b_ref, o_ref):
    c_rows = jnp.concatenate([jnp.broadcast_to(c_ref[i:i + 1, :], (ROWS_F32, c_ref.shape[1]))
                              for i in range(c_ref.shape[0])], axis=0)
    o_ref[...] = jnp.dot(c_rows.astype(_BF16), w_ref[...].astype(_BF16),
                         preferred_element_type=_F32) + b_ref[...]


def _adaln_mod(c, w_ada, b_ada):
    batch, d = c.shape
    n = w_ada.shape[1]
    tn = ADALN_COLS
    return pl.pallas_call(
        _adaln_kernel,
        grid=(n // tn,),
        in_specs=[pl.BlockSpec((batch, d), lambda j: (0, 0)),
                  pl.BlockSpec((d, tn), lambda j: (0, j)),
                  pl.BlockSpec((1, tn), lambda j: (0, j))],
        out_specs=pl.BlockSpec((batch * ROWS_F32, tn), lambda j: (0, j)),
        out_shape=jax.ShapeDtypeStruct((batch * ROWS_F32, n), _F32),
        compiler_params=pltpu.CompilerParams(vmem_limit_bytes=VMEM_LIMIT_BYTES),
        name="adaln_mod",
    )(c, w_ada, b_ada)


def _mod_row(mod_ref, step, steps_per_batch):
    first = pl.multiple_of((step // steps_per_batch) * ROWS_F32, ROWS_F32)
    return mod_ref[pl.ds(first, ROWS_F32), :][0:1, :]


_G_U, _G_V, _G_ZA, _G_Q, _G_K, _G_VB, _G_ZB = range(7)


def _inproj_kernel(x_ref, mod_ref, w32_ref, lng_ref, lnb_ref, ws_ref, bs_ref,
                   oa_ref, qt_ref, k_ref, vt_ref, sb_ref, w_ref, wt_ref, *,
                   d_model, width, steps_per_batch):
    @pl.when(pl.program_id(0) == 0)
    def _():
        def cast_rows(r, carry):
            rows = pl.ds(pl.multiple_of(r * BLK, BLK), BLK)
            w_ref[rows, :] = w32_ref[rows, :].astype(_BF16)
            return carry
        lax.fori_loop(0, d_model // BLK, cast_rows, 0)
        for i, g in enumerate((_G_Q, _G_VB)):
            for c0 in range(0, width, BLK):
                wt_ref[i * width + c0:i * width + c0 + BLK, :] = (
                    w32_ref[:, g * width + c0:g * width + c0 + BLK].T.astype(_BF16))

    mod = _mod_row(mod_ref, pl.program_id(0), steps_per_batch)
    shift = mod[:, :d_model]
    scale = 1.0 + mod[:, d_model:2 * d_model]

    row = lax.broadcasted_iota(jnp.int32, (BLK, BLK), 0)
    col = lax.broadcasted_iota(jnp.int32, (BLK, BLK), 1)
    causal = (col // CHUNK) <= (row // CHUNK)
    heads_per_lane_group = LANES // HEAD_DIM
    w_pairs = [jnp.concatenate(
        [jnp.where(causal, ws_ref[heads_per_lane_group * p + i], 0.0).astype(_BF16)
         for i in range(heads_per_lane_group)], axis=1) for p in range(width // LANES)]
    head_of_lane = lax.broadcasted_iota(jnp.int32, (BLK, LANES), 1) // HEAD_DIM
    zero_tile = jnp.zeros((BLK, LANES), _BF16)
    blocks_per_dot = 2
    q_scale = 1.0 / math.sqrt(HEAD_DIM)

    for s0 in range(0, x_ref.shape[0], IN_PROJ_SUB_ROWS):
        sub = slice(s0, s0 + IN_PROJ_SUB_ROWS)
        h = (x_ref[sub, :] * scale + shift).astype(_BF16)

        def group(g):
            return jnp.dot(h, w_ref[:, g * width:(g + 1) * width], preferred_element_type=_F32)

        vn = _layer_norm_rows(jax.nn.gelu(group(_G_V)), lng_ref[...], lnb_ref[...]).astype(_BF16)
        u = jax.nn.gelu(group(_G_U))
        za = group(_G_ZA)
        gate_a = u * (za * jax.nn.sigmoid(za))

        k_ref[sub, :] = group(_G_K).astype(_BF16)
        zb = group(_G_ZB)
        sb_ref[sub, :] = zb * jax.nn.sigmoid(zb)
        qv_t = lax.dot_general(wt_ref[...], h, _NT, preferred_element_type=_F32)
        for r in range(IN_PROJ_SUB_ROWS // BLK):
            cols = slice(r * BLK, (r + 1) * BLK)
            qt_ref[0, s0 // BLK + r] = (qv_t[:width, cols] * q_scale).astype(_BF16)
            vt_ref[0, s0 // BLK + r] = qv_t[width:, cols].astype(_BF16)

        for p in range(width // LANES):
            cols = slice(p * LANES, (p + 1) * LANES)
            for r0 in range(0, IN_PROJ_SUB_ROWS // BLK, blocks_per_dot):
                blocks = [slice(r * BLK, (r + 1) * BLK) for r in range(r0, r0 + blocks_per_dot)]
                stacked = jnp.concatenate(
                    [jnp.concatenate([jnp.where(head_of_lane == i, vn[rows, cols], zero_tile)
                                      for i in range(heads_per_lane_group)], axis=0)
                     for rows in blocks], axis=1)
                res = jnp.dot(w_pairs[p], stacked, preferred_element_type=_F32)
                for i, rows in enumerate(blocks):
                    mixed = res[:, i * LANES:(i + 1) * LANES] + bs_ref[:, cols]
                    oa_ref[s0 + rows.start:s0 + rows.stop, cols] = (gate_a[rows, cols] * mixed).astype(_BF16)


def _in_proj(x2, mod, w_in, ln_v_g, ln_v_b, w_spatial, bias_tile, *, batch, seq):
    m, d_model = x2.shape
    width = ln_v_g.shape[-1]
    tm = IN_PROJ_ROWS
    steps_per_batch = seq // tm
    nblk = seq // BLK
    row_spec = lambda cols: pl.BlockSpec((tm, cols), lambda i: (i, 0))
    full = lambda a: pl.BlockSpec(a.shape, lambda i: (0,) * a.ndim)
    t_spec = pl.BlockSpec((1, tm // BLK, width, BLK),
                          lambda i: (i // steps_per_batch, i % steps_per_batch, 0, 0))
    return pl.pallas_call(
        functools.partial(_inproj_kernel, d_model=d_model, width=width,
                          steps_per_batch=steps_per_batch),
        grid=(m // tm,),
        in_specs=[row_spec(d_model), full(mod),
                  pl.BlockSpec(w_in.shape, lambda i: (0, 0), pipeline_mode=pl.Buffered(1)),
                  full(ln_v_g), full(ln_v_b), full(w_spatial), full(bias_tile)],
        out_specs=[row_spec(width), t_spec, row_spec(width), t_spec, row_spec(width)],
        out_shape=[jax.ShapeDtypeStruct((m, width), _BF16),
                   jax.ShapeDtypeStruct((batch, nblk, width, BLK), _BF16),
                   jax.ShapeDtypeStruct((m, width), _BF16),
                   jax.ShapeDtypeStruct((batch, nblk, width, BLK), _BF16),
                   jax.ShapeDtypeStruct((m, width), _F32)],
        scratch_shapes=[pltpu.VMEM(w_in.shape, _BF16),
                        pltpu.VMEM((2 * width, d_model), _BF16)],
        compiler_params=pltpu.CompilerParams(dimension_semantics=("arbitrary",),
                                             vmem_limit_bytes=VMEM_LIMIT_BYTES),
        name="in_proj",
    )(x2, mod, w_in, ln_v_g, ln_v_b, w_spatial, bias_tile)


def _sb_softplus(z):
    return jnp.maximum(z, 0.0) + jnp.log(1.0 + jnp.exp2(jnp.abs(z) * (-_LOG2E)))


def _bf16_split(v):
    hi = v.astype(_BF16)
    lo = (v - hi.astype(_F32)).astype(_BF16)
    return jnp.concatenate([hi, lo], axis=0)


def _attn_kernel(qt_ref, k_ref, vt_ref, sb_ref, o_ref, acc_ref, dec_ref, *, exit_floor):
    nqb = qt_ref.shape[1]
    width = qt_ref.shape[2]
    n_groups = width // LANES
    pair = 2 * LANES
    half = BLK // 2
    first_qblock = pl.program_id(1) * nqb

    def cum_matrix(nk):
        row = lax.broadcasted_iota(jnp.int32, (nk, nk), 0)
        col = lax.broadcasted_iota(jnp.int32, (nk, nk), 1)
        m = jnp.where(col >= row, 1.0, 0.0).astype(_BF16)
        return jnp.concatenate([m, m], axis=1)

    cum = {BLK: cum_matrix(BLK), half: cum_matrix(half)}
    causal_bias = jnp.where(lax.broadcasted_iota(jnp.int32, (BLK, pair), 0)
                            < (lax.broadcasted_iota(jnp.int32, (BLK, pair), 1) & (LANES - 1)),
                            0.0, _MASKED_SCORE)
    last_row = lax.broadcasted_iota(jnp.int32, (ROWS_F32, pair), 0) == ROWS_F32 - 1
    zero_rows = jnp.zeros((HEAD_DIM, LANES), _BF16)
    zero_half = jnp.zeros((half, pair), _BF16)
    lanes_of = lambda g: slice(g * LANES, (g + 1) * LANES)

    def q_block_diag(qb, g):
        q = qt_ref[0, qb, lanes_of(g), :]
        left = jnp.concatenate([q[:HEAD_DIM], zero_rows], axis=0)
        right = jnp.concatenate([zero_rows, q[HEAD_DIM:]], axis=0)
        return jnp.concatenate([left, right], axis=1)

    def sweep(jobs):
        chains, q_bd, dec, acc = [], {}, {}, {}
        for qb, spans, decay in jobs:
            for g in range(n_groups):
                q_bd[qb, g] = q_block_diag(qb, g)
                dec[qb, g] = None if decay is None else decay[g]
        for d in range(max(len(spans) for _, spans, _ in jobs)):
            for qb, spans, _ in jobs:
                if d < len(spans):
                    chains.extend((qb, g) + spans[d] for g in range(n_groups))
        z, w = {}, {}

        def scores(c):
            qb, g, jc, row0, nk, present, diagonal = chains[c]
            rows = slice(row0, row0 + nk) if isinstance(row0, int) else pl.ds(row0, nk)
            zc = jnp.dot(k_ref[0, jc, rows, lanes_of(g)], q_bd[qb, g], preferred_element_type=_F32)
            z[c] = zc + causal_bias if diagonal else zc

        def weights(c):
            qb, g, jc, row0, nk, present, diagonal = chains[c]
            sp = _sb_softplus(z[c])
            before = dec[qb, g]
            if before is not None:
                before = before if present is None else jnp.where(present, before, -_MASKED_SCORE)
                sp = jnp.concatenate([sp[:nk - ROWS_F32],
                                      sp[nk - ROWS_F32:] + jnp.where(last_row, before, 0.0)], axis=0)
            cs = jnp.dot(cum[nk], _bf16_split(sp), preferred_element_type=_F32)
            wc = jnp.exp(z.pop(c) - cs).astype(_BF16)
            if nk == half:
                if isinstance(row0, int):
                    wc = jnp.concatenate([zero_half, wc] if row0 else [wc, zero_half], axis=0)
                else:
                    wc = jnp.where(row0 > 0, jnp.concatenate([zero_half, wc], axis=0),
                                   jnp.concatenate([wc, zero_half], axis=0))
            w[c] = wc
            total = jnp.broadcast_to(cs[0:1, :], (ROWS_F32, pair))
            if before is not None:
                total = total - before
            if present is not None:
                total = jnp.where(present, total, 0.0)
            dec[qb, g] = total if dec[qb, g] is None else dec[qb, g] + total

        def values(c):
            qb, g, jc = chains[c][:3]
            both = jnp.dot(vt_ref[0, jc, lanes_of(g), :], w.pop(c), preferred_element_type=_F32)
            pv = jnp.concatenate([both[:HEAD_DIM, :LANES], both[HEAD_DIM:, LANES:]], axis=0)
            acc[qb, g] = pv if acc.get((qb, g)) is None else acc[qb, g] + pv

        stages = (scores, weights, values)
        for step in range(len(chains) + ATTN_SKEW * (len(stages) - 1)):
            for s, stage in enumerate(stages):
                c = step - ATTN_SKEW * s
                if 0 <= c < len(chains):
                    stage(c)
        return [([acc[qb, g] for g in range(n_groups)], [dec[qb, g] for g in range(n_groups)])
                for qb, _, _ in jobs]

    pos_inf = jnp.full((ROWS_F32, pair), jnp.inf, _F32)
    q_index = lambda qb: first_qblock + qb

    def static_spans(i_q):
        spans = [(i_q, 0, BLK, None, True)]
        for d in range(1, ATTN_STATIC_HALF_BLOCKS // 2):
            spans.append((jnp.maximum(i_q - d, 0), 0, BLK, i_q - d >= 0, False))
        d = ATTN_STATIC_HALF_BLOCKS // 2
        spans.append((jnp.maximum(i_q - d, 0), half, half, i_q - d >= 0, False))
        return spans

    out = sweep([(qb, static_spans(q_index(qb)), None) for qb in range(nqb)])

    def commit(out, more, accumulate):
        least = pos_inf
        for qb, (pv, decay) in enumerate(out):
            for g in range(n_groups):
                if accumulate:
                    acc_ref[qb, lanes_of(g), :] += pv[g]
                else:
                    acc_ref[qb, lanes_of(g), :] = pv[g]
                dec_ref[qb * n_groups + g] = decay[g]
                least = jnp.minimum(least, jnp.where(more[qb], decay[g], pos_inf))
        return jnp.min(least)

    def write_output():
        for qb in range(nqb):
            rows = slice(qb * BLK, (qb + 1) * BLK)
            for g in range(n_groups):
                o_ref[0, rows, lanes_of(g)] = (acc_ref[qb, lanes_of(g), :].T
                                               * sb_ref[0, rows, lanes_of(g)]).astype(_BF16)

    next_half = lambda qb, n: 2 * q_index(qb) - (ATTN_STATIC_HALF_BLOCKS - 1) - n
    least = commit(out, [next_half(qb, 0) >= 0 for qb in range(nqb)], False)
    write_output()

    def tail_cond(state):
        return -state[1] > exit_floor

    def tail_body(state):
        n = state[0]
        jobs = []
        for qb in range(nqb):
            hb = next_half(qb, n)
            hbc = jnp.maximum(hb, 0)
            row0 = pl.multiple_of((hbc & 1) * half, half)
            jobs.append((qb, [(hbc >> 1, row0, half, hb >= 0, False)],
                         [dec_ref[qb * n_groups + g] for g in range(n_groups)]))
        return n + 1, commit(sweep(jobs), [next_half(qb, n) >= 1 for qb in range(nqb)], True)

    tail_iterations, _ = lax.while_loop(tail_cond, tail_body, (jnp.int32(0), least))

    @pl.when(tail_iterations > 0)
    def _():
        write_output()


def _sb_attention(q_t, k_blocks, v_t, sb3, *, exit_floor=ATTN_LOG_KEEP_FLOOR):
    batch, nblk, width, _ = q_t.shape
    nqb = ATTN_QBLOCKS
    tq = nqb * BLK
    resident = lambda a: pl.BlockSpec((1,) + a.shape[1:], lambda b, i: (b, 0, 0, 0))
    return pl.pallas_call(
        functools.partial(_attn_kernel, exit_floor=exit_floor),
        grid=(batch, nblk // nqb),
        in_specs=[pl.BlockSpec((1, nqb, width, BLK), lambda b, i: (b, i, 0, 0)),
                  resident(k_blocks), resident(v_t),
                  pl.BlockSpec((1, tq, width), lambda b, i: (b, i, 0))],
        out_specs=pl.BlockSpec((1, tq, width), lambda b, i: (b, i, 0)),
        out_shape=jax.ShapeDtypeStruct(sb3.shape, _BF16),
        scratch_shapes=[pltpu.VMEM((nqb, width, BLK), _F32),
                        pltpu.VMEM((nqb * width // LANES, ROWS_F32, 2 * LANES), _F32)],
        compiler_params=pltpu.CompilerParams(
            dimension_semantics=("arbitrary", "arbitrary"),
            vmem_limit_bytes=VMEM_LIMIT_BYTES),
        name="sb_attn",
    )(q_t, k_blocks, v_t, sb3)


def _outproj_kernel(oa_ref, ob_ref, x_ref, mod_ref, w_ref, g_ref, b_ref, o_ref, wb_ref, *,
                    alpha, steps_per_batch):
    width = oa_ref.shape[1]
    d_model = x_ref.shape[1]

    @pl.when(pl.program_id(0) == 0)
    def _():
        wb_ref[...] = w_ref[...].astype(_BF16)

    gate = 1.0 + _mod_row(mod_ref, pl.program_id(0), steps_per_batch)[:, 2 * d_model:]
    for r0 in range(0, x_ref.shape[0], OUT_PROJ_SUB_ROWS):
        rows = slice(r0, r0 + OUT_PROJ_SUB_ROWS)
        y = (jnp.dot(oa_ref[rows, :], wb_ref[:width, :], preferred_element_type=_F32)
             + jnp.dot(ob_ref[rows, :], wb_ref[width:, :], preferred_element_type=_F32))
        r = alpha * x_ref[rows, :] + gate * y
        o_ref[rows, :] = _layer_norm_rows(r, g_ref[...], b_ref[...])


def _out_proj(out_a, out_b, x2, mod, w_out, ln_g, ln_b, *, seq, alpha):
    m, d_model = x2.shape
    width = out_a.shape[1]
    tm = OUT_PROJ_ROWS
    steps_per_batch = seq // tm
    row_spec = lambda cols: pl.BlockSpec((tm, cols), lambda i: (i, 0))
    full = lambda a: pl.BlockSpec(a.shape, lambda i: (0,) * a.ndim)
    return pl.pallas_call(
        functools.partial(_outproj_kernel, alpha=alpha, steps_per_batch=steps_per_batch),
        grid=(m // tm,),
        in_specs=[row_spec(width), row_spec(width), row_spec(d_model), full(mod),
                  pl.BlockSpec(w_out.shape, lambda i: (0, 0), pipeline_mode=pl.Buffered(1)),
                  full(ln_g), full(ln_b)],
        out_specs=row_spec(d_model),
        out_shape=jax.ShapeDtypeStruct((m, d_model), _F32),
        scratch_shapes=[pltpu.VMEM(w_out.shape, _BF16)],
        compiler_params=pltpu.CompilerParams(dimension_semantics=("arbitrary",),
                                             vmem_limit_bytes=VMEM_LIMIT_BYTES),
        name="out_proj",
    )(out_a, out_b, x2, mod, w_out, ln_g, ln_b)


def kernel(x, c, w_ada, b_ada, w_in, ln_v_g, ln_v_b, w_spatial, b_spatial, w_out, ln_g, ln_b):
    batch, seq, d_model = x.shape
    depth = w_ada.shape[0]
    width = ln_v_g.shape[-1]
    n_heads = w_spatial.shape[1]
    assert width == n_heads * HEAD_DIM and w_in.shape[-1] == 7 * width
    assert w_spatial.shape[2:] == (BLK, BLK)
    assert seq % IN_PROJ_ROWS == 0 and seq % OUT_PROJ_ROWS == 0
    assert IN_PROJ_ROWS % IN_PROJ_SUB_ROWS == 0 and IN_PROJ_SUB_ROWS % (2 * BLK) == 0
    assert (seq // BLK) % ATTN_QBLOCKS == 0 and (3 * d_model) % ADALN_COLS == 0
    alpha = (2 * depth) ** 0.25

    x2 = x.reshape(batch * seq, d_model)
    for l in range(depth):
        bias_tile = jnp.repeat(b_spatial[l].T, HEAD_DIM, axis=1)
        mod = _adaln_mod(c, w_ada[l], b_ada[l][None, :])
        out_a, q_t, k, v_t, sb = _in_proj(
            x2, mod, w_in[l], ln_v_g[l][None, :], ln_v_b[l][None, :], w_spatial[l], bias_tile,
            batch=batch, seq=seq)
        out_b = _sb_attention(q_t, k.reshape(batch, seq // BLK, BLK, width), v_t,
                              sb.reshape(batch, seq, width))
        x2 = _out_proj(out_a, out_b.reshape(batch * seq, width), x2, mod, w_out[l],
                       ln_g[l][None, :], ln_b[l][None, :], seq=seq, alpha=alpha)
    return x2.reshape(batch, seq, d_model)
```
